```python
import math
import jax, jax.numpy as jnp
from jax import lax
import numpy as np

D_MODEL = 1024
BATCH = 8
SEQ = 2048
DEPTH = 4
DEC_BATCH = 128
DEC_SEQ = 4
PAST_LEN = 16384
PAGE_SIZE = 128

LRU_WIDTH = D_MODEL
LRU_BLOCKS = 4
LRU_BW = LRU_WIDTH // LRU_BLOCKS
CONV_W = 4
LRU_C = 8.0
RET_HEADS = 4
RET_DK = D_MODEL // RET_HEADS
RET_DV = D_MODEL // RET_HEADS
ROPE_BASE = 10000.0
HG_DK = 128
HG_HEADS = D_MODEL // HG_DK
HG_DV = D_MODEL // HG_HEADS
D_FF = ((-(-8 * D_MODEL // 3) + 255) // 256) * 256
CHUNK = 64
EPS = 1e-6
GN_EPS = 1e-5
IN_WIDTHS = (LRU_WIDTH, LRU_WIDTH,
             RET_HEADS * RET_DK, RET_HEADS * RET_DK, RET_HEADS * RET_DV, RET_HEADS * RET_DV,
             HG_HEADS * HG_DK, HG_HEADS * HG_DK, HG_HEADS * HG_DV, HG_HEADS * HG_DV,
             D_MODEL, D_MODEL, D_MODEL)
IN_COLS = sum(IN_WIDTHS)

kernel_name = 'hybrid_rglru_retention_hgrn2_step'

F32 = jnp.float32


def rmsnorm(x, w):
    x = x.astype(F32)
    return x * lax.rsqrt(jnp.mean(x * x, axis=-1, keepdims=True) + EPS) * w.astype(F32)


def rms_heads(o):
    return o * lax.rsqrt(jnp.mean(o * o, axis=-1, keepdims=True) + EPS)


def group_norm_heads(o):
    mu = jnp.mean(o, axis=-1, keepdims=True)
    d = o - mu
    return d * lax.rsqrt(jnp.mean(d * d, axis=-1, keepdims=True) + GN_EPS)


def causal_conv(x, buf, w, b):
    L = x.shape[1]
    xp = jnp.concatenate([buf.astype(F32), x], axis=1)
    y = b.astype(F32)
    for j in range(CONV_W):
        y = y + w[j].astype(F32) * xp[:, j:j + L]
    return y, xp[:, L:]


def _lin_comb(e1, e2):
    a1, b1 = e1
    a2, b2 = e2
    return a1 * a2, a2 * b1 + b2


def rglru(xc, pos, h0, wa, ba, wx, bx, lam):
    B, L, W = xc.shape
    xb = xc.reshape(B, L, LRU_BLOCKS, LRU_BW)
    r = jax.nn.sigmoid(jnp.einsum('blnc,ncd->blnd', xb, wa.astype(F32)).reshape(B, L, W) + ba)
    i = jax.nn.sigmoid(jnp.einsum('blnc,ncd->blnd', xb, wx.astype(F32)).reshape(B, L, W) + bx)
    log_a = -LRU_C * r * jax.nn.softplus(-lam.astype(F32))
    a = jnp.exp(log_a)
    mult = jnp.sqrt(-jnp.expm1(2.0 * log_a))
    mult = jnp.where((pos == 0)[None, :, None], 1.0, mult)
    u = xc * i * mult
    u = u.at[:, 0].add(a[:, 0] * h0.astype(F32))
    _, h = lax.associative_scan(_lin_comb, (a, u), axis=1)
    return h, h[:, -1]


def rotary(x, pos):
    half = x.shape[-1] // 2
    inv = ROPE_BASE ** (-jnp.arange(half, dtype=F32) / half)
    ang = pos.astype(F32)[:, None] * inv[None, :]
    cos = jnp.cos(ang)[None, :, None, :]
    sin = jnp.sin(ang)[None, :, None, :]
    x1, x2 = x[..., :half], x[..., half:]
    return jnp.concatenate([x1 * cos - x2 * sin, x1 * sin + x2 * cos], axis=-1)


def to_chunks(z, chunk):
    B, L = z.shape[:2]
    return z.reshape(B, L // chunk, chunk, *z.shape[2:]).swapaxes(0, 1)


def from_chunks(z):
    N, B, C = z.shape[:3]
    return z.swapaxes(0, 1).reshape(B, N * C, *z.shape[3:])


def retention_chunked(q, k, v, R0, chunk):
    H = q.shape[2]
    log_g = jnp.log1p(-jnp.exp2(-5.0 - jnp.arange(H, dtype=F32)))
    t = jnp.arange(chunk, dtype=F32)
    rel = t[:, None] - t[None, :]
    D = jnp.exp(jnp.where(rel[None] >= 0, rel[None] * log_g[:, None, None], -jnp.inf))
    xi = jnp.exp((t[None, :] + 1.0) * log_g[:, None]).T
    zeta = jnp.exp((chunk - 1.0 - t[None, :]) * log_g[:, None]).T
    g_c = jnp.exp(chunk * log_g)

    def step(R, inp):
        qc, kc, vc = inp
        s = jnp.einsum('bthd,bshd->bhts', qc, kc) * D[None]
        o = (jnp.einsum('bhts,bshe->bthe', s, vc)
             + jnp.einsum('bthd,bhde->bthe', qc, R) * xi[None, :, :, None])
        R = g_c[None, :, None, None] * R + jnp.einsum('bshd,bshe->bhde', kc * zeta[None, :, :, None], vc)
        return R, o

    R, o = lax.scan(step, R0, (to_chunks(q, chunk), to_chunks(k, chunk), to_chunks(v, chunk)))
    return from_chunks(o), R


def hgrn2_chunked(q, logf, v, S0, chunk):
    k = -jnp.expm1(logf)
    tri = jnp.tril(jnp.ones((chunk, chunk), dtype=bool))[None, :, :, None, None]

    def step(S, inp):
        qc, gc, kc, vc = inp
        b = jnp.cumsum(gc, axis=1)
        o_inter = jnp.einsum('bthd,bhde->bthe', qc * jnp.exp(b), S)
        decay = jnp.exp(jnp.where(tri, b[:, :, None] - b[:, None, :], -jnp.inf))
        A = jnp.einsum('bthd,bshd,btshd->btsh', qc, kc, decay)
        o = o_inter + jnp.einsum('btsh,bshe->bthe', A, vc)
        b_end = b[:, -1]
        S = (jnp.exp(b_end)[..., None] * S
             + jnp.einsum('bshd,bshe->bhde', kc * jnp.exp(b_end[:, None] - b), vc))
        return S, o

    S, o = lax.scan(step, S0, (to_chunks(q, chunk), to_chunks(logf, chunk),
                               to_chunks(k, chunk), to_chunks(v, chunk)))
    return from_chunks(o), S


def run_trunk(x, pos, conv_st, lru_st, ret_st, hg_st, norm_mix_w, w_in, conv_w, conv_b,
              lru_wa, lru_ba, lru_wx, lru_bx, lru_lambda, lb_all, hg_norm_w, w_out,
              norm_ffn_w, w_gate, w_up, w_down, norm_final_w):
    h = x.astype(F32)
    B, L, _ = h.shape
    chunk = math.gcd(L, CHUNK)
    offs = np.cumsum(IN_WIDTHS)[:-1].tolist()
    n_conv, n_lru, n_ret, n_hg = [], [], [], []
    for l in range(DEPTH):
        u = rmsnorm(h, norm_mix_w[l])
        z = u @ w_in[l]
        (a_x, a_y, r_q, r_k, r_v, r_g, c_q, c_f, c_i, c_g,
         m_a, m_b, m_c) = jnp.split(z, offs, axis=-1)
        xc, cbuf = causal_conv(a_x, conv_st[l], conv_w[l], conv_b[l])
        hA, h_last = rglru(xc, pos, lru_st[l], lru_wa[l], lru_ba[l], lru_wx[l], lru_bx[l], lru_lambda[l])
        o_a = jax.nn.gelu(a_y) * hA
        q = rotary(r_q.reshape(B, L, RET_HEADS, RET_DK), pos)
        k = rotary(r_k.reshape(B, L, RET_HEADS, RET_DK), pos) * RET_DK ** -0.5
        v = r_v.reshape(B, L, RET_HEADS, RET_DV)
        o_r, R = retention_chunked(q, k, v, ret_st[l].astype(F32), chunk)
        o_b = group_norm_heads(o_r).reshape(B, L, RET_HEADS * RET_DV) * jax.nn.silu(r_g)
        if l == 0:
            logf = jax.nn.log_sigmoid(c_f)
        else:
            lb = lb_all[l]
            logf = jnp.log(lb + (1.0 - lb) * jax.nn.sigmoid(c_f))
        o_h, S = hgrn2_chunked(jax.nn.silu(c_q).reshape(B, L, HG_HEADS, HG_DK),
                               logf.reshape(B, L, HG_HEADS, HG_DK),
                               c_i.reshape(B, L, HG_HEADS, HG_DV), hg_st[l].astype(F32), chunk)
        o_c = rms_heads(o_h).reshape(B, L, HG_HEADS * HG_DV) * hg_norm_w[l] * jax.nn.silu(c_g)
        mix = jax.nn.sigmoid(m_a) * o_a + jax.nn.sigmoid(m_b) * o_b + jax.nn.sigmoid(m_c) * o_c
        h = h + mix @ w_out[l]
        u = rmsnorm(h, norm_ffn_w[l])
        h = h + (jax.nn.silu(u @ w_gate[l]) * (u @ w_up[l])) @ w_down[l]
        n_conv.append(cbuf)
        n_lru.append(h_last)
        n_ret.append(R)
        n_hg.append(S)
    y = rmsnorm(h, norm_final_w)
    return y, jnp.stack(n_conv), jnp.stack(n_lru), jnp.stack(n_ret), jnp.stack(n_hg)


def setup_inputs(seed: int = 0) -> dict:
    key = jax.random.key(seed)
    ks = jax.random.split(key, 24)

    def nrm(k, shape, scale):
        return jax.random.normal(k, shape, F32) * scale

    u = jax.random.uniform(ks[14], (DEPTH, LRU_WIDTH), F32, 0.9, 0.999)
    a0 = u ** (1.0 / LRU_C)
    lru_lambda = jnp.log(a0) - jnp.log1p(-a0)
    return {
        'x_prompt': nrm(ks[0], (BATCH, SEQ, D_MODEL), 1.0),
        'x_sample': nrm(ks[1], (DEC_BATCH, DEC_SEQ, D_MODEL), 1.0),
        'state_conv': nrm(ks[2], (DEPTH, DEC_BATCH, CONV_W - 1, LRU_WIDTH), 1.0),
        'state_lru': nrm(ks[3], (DEPTH, DEC_BATCH, LRU_WIDTH), 0.5),
        'state_ret': nrm(ks[4], (DEPTH, DEC_BATCH, RET_HEADS, RET_DK, RET_DV), 0.5),
        'state_hgrn': nrm(ks[5], (DEPTH, DEC_BATCH, HG_HEADS, HG_DK, HG_DV), 0.5),
        'norm_mix_w': 1.0 + nrm(ks[6], (DEPTH, D_MODEL), 0.01),
        'w_in': nrm(ks[7], (DEPTH, D_MODEL, IN_COLS), D_MODEL ** -0.5),
        'conv_w': nrm(ks[8], (DEPTH, CONV_W, LRU_WIDTH), CONV_W ** -0.5),
        'conv_b': nrm(ks[9], (DEPTH, LRU_WIDTH), 0.01),
        'lru_wa': nrm(ks[10], (DEPTH, LRU_BLOCKS, LRU_BW, LRU_BW), LRU_BW ** -0.5),
        'lru_ba': nrm(ks[11], (DEPTH, LRU_WIDTH), 0.01),
        'lru_wx': nrm(ks[12], (DEPTH, LRU_BLOCKS, LRU_BW, LRU_BW), LRU_BW ** -0.5),
        'lru_bx': nrm(ks[13], (DEPTH, LRU_WIDTH), 0.01),
        'lru_lambda': lru_lambda,
        'hg_lb': nrm(ks[15], (DEPTH, HG_HEADS * HG_DK), 0.1),
        'hg_norm_w': 1.0 + nrm(ks[16], (DEPTH, HG_HEADS * HG_DV), 0.01),
        'w_out': nrm(ks[17], (DEPTH, D_MODEL, D_MODEL), D_MODEL ** -0.5),
        'norm_ffn_w': 1.0 + nrm(ks[18], (DEPTH, D_MODEL), 0.01),
        'w_gate': nrm(ks[19], (DEPTH, D_MODEL, D_FF), D_MODEL ** -0.5),
        'w_up': nrm(ks[20], (DEPTH, D_MODEL, D_FF), D_MODEL ** -0.5),
        'w_down': nrm(ks[21], (DEPTH, D_FF, D_MODEL), D_FF ** -0.5),
        'norm_final_w': 1.0 + nrm(ks[22], (D_MODEL,), 0.01),
    }


def reference(x_prompt, x_sample, state_conv, state_lru, state_ret, state_hgrn, norm_mix_w, w_in,
              conv_w, conv_b, lru_wa, lru_ba, lru_wx, lru_bx, lru_lambda, hg_lb, hg_norm_w, w_out,
              norm_ffn_w, w_gate, w_up, w_down, norm_final_w):
    sm = jax.nn.softmax(hg_lb.astype(F32), axis=0)
    cs = jnp.cumsum(sm, axis=0)
    lb_all = cs - cs[0:1]
    B = x_prompt.shape[0]
    z_conv = jnp.zeros((DEPTH, B, CONV_W - 1, LRU_WIDTH), state_conv.dtype)
    z_lru = jnp.zeros((DEPTH, B, LRU_WIDTH), state_lru.dtype)
    z_ret = jnp.zeros((DEPTH, B, RET_HEADS, RET_DK, RET_DV), state_ret.dtype)
    z_hg = jnp.zeros((DEPTH, B, HG_HEADS, HG_DK, HG_DV), state_hgrn.dtype)
    pos_p = jnp.arange(x_prompt.shape[1], dtype=jnp.int32)
    pos_s = PAST_LEN + jnp.arange(x_sample.shape[1], dtype=jnp.int32)
    y_p, conv_p, lru_p, ret_p, hg_p = run_trunk(
        x_prompt, pos_p, z_conv, z_lru, z_ret, z_hg, norm_mix_w, w_in, conv_w, conv_b,
        lru_wa, lru_ba, lru_wx, lru_bx, lru_lambda, lb_all, hg_norm_w, w_out,
        norm_ffn_w, w_gate, w_up, w_down, norm_final_w)
    y_s, conv_s, lru_s, ret_s, hg_s = run_trunk(
        x_sample, pos_s, state_conv, state_lru, state_ret, state_hgrn, norm_mix_w, w_in, conv_w,
        conv_b, lru_wa, lru_ba, lru_wx, lru_bx, lru_lambda, lb_all, hg_norm_w, w_out,
        norm_ffn_w, w_gate, w_up, w_down, norm_final_w)
    return (y_p.astype(x_prompt.dtype), y_s.astype(x_sample.dtype),
            conv_p.astype(state_conv.dtype), lru_p.astype(state_lru.dtype),
            ret_p.astype(state_ret.dtype), hg_p.astype(state_hgrn.dtype),
            conv_s.astype(state_conv.dtype), lru_s.astype(state_lru.dtype),
            ret_s.astype(state_ret.dtype), hg_s.astype(state_hgrn.dtype))
```

```python
import functools
import math

import numpy as np
import jax
import jax.numpy as jnp
from jax import lax
from jax.experimental import pallas as pl
from jax.experimental.pallas import tpu as pltpu

F32 = jnp.float32
BF16 = jnp.bfloat16

D_MODEL = 1024
DEPTH = 4
PAST_LEN = 16384
LRU_BLOCKS = 4
LRU_BW = D_MODEL // LRU_BLOCKS
CONV_W = 4
LRU_C = 8.0
RET_HEADS = 4
RET_DK = 256
RET_DV = 256
ROPE_BASE = 10000.0
HG_DK = 128
HG_HEADS = 8
HG_DV = 128
D_FF = 2816
EPS = 1e-6
GN_EPS = 1e-5
N_GROUPS = 13
IN_COLS = N_GROUPS * D_MODEL
G_AX, G_AY, G_RQ, G_RK, G_RV, G_RG, G_CQ, G_CF, G_CI, G_CG, G_MA, G_MB, G_MC = range(13)

VMEM_LIMIT = 56 * 1024 * 1024


def _cparams(n_axes):
    return pltpu.CompilerParams(dimension_semantics=("arbitrary",) * n_axes,
                                vmem_limit_bytes=VMEM_LIMIT)


def _rms(x, w):
    return x * lax.rsqrt(jnp.mean(x * x, axis=-1, keepdims=True) + EPS) * w


def _dot(a, b):
    return jnp.dot(a, b, preferred_element_type=F32)


def _dot_nt(a, b):
    return lax.dot_general(a, b, (((1,), (1,)), ((), ())), preferred_element_type=F32)


def _dot_tn(a, b):
    return lax.dot_general(a, b, (((0,), (0,)), ((), ())), preferred_element_type=F32)


def _proj_body(h_ref, nw_ref, w_ref, z_ref, u_scr):
    @pl.when(pl.program_id(1) == 0)
    def _():
        u_scr[...] = _rms(h_ref[...], nw_ref[...]).astype(BF16)

    z_ref[...] = _dot(u_scr[...], w_ref[...])


def _proj(h, nw_all, w_all, layer, tm, tn):
    n = h.shape[0]
    ncols = w_all.shape[2]
    return pl.pallas_call(
        _proj_body,
        grid=(n // tm, ncols // tn),
        in_specs=[
            pl.BlockSpec((tm, D_MODEL), lambda i, j: (i, 0)),
            pl.BlockSpec((None, 1, D_MODEL), lambda i, j: (layer, 0, 0)),
            pl.BlockSpec((None, D_MODEL, tn), lambda i, j: (layer, 0, j)),
        ],
        out_specs=pl.BlockSpec((tm, tn), lambda i, j: (i, j)),
        out_shape=jax.ShapeDtypeStruct((n, ncols), F32),
        scratch_shapes=[pltpu.VMEM((tm, D_MODEL), BF16)],
        compiler_params=_cparams(2),
        name="proj",
    )(h, nw_all, w_all)


def _mixa_body(*refs, T, NB, pos0, has_init):
    if has_init:
        (zx_ref, zy_ref, zm_ref, cw_ref, cb_ref, wa_ref, ba_ref, wx_ref, bx_ref, lam_ref,
         conv0_ref, lru0_ref, out_ref, conv_out_ref, lru_out_ref,
         xpad, hcar, a_scr, u_scr, h_scr) = refs
    else:
        (zx_ref, zy_ref, zm_ref, cw_ref, cb_ref, wa_ref, ba_ref, wx_ref, bx_ref, lam_ref,
         out_ref, conv_out_ref, lru_out_ref,
         xpad, hcar, a_scr, u_scr, h_scr) = refs
    t = pl.program_id(1)

    @pl.when(t == 0)
    def _():
        if has_init:
            xpad[:, 5:8, :] = conv0_ref[...]
            hcar[...] = lru0_ref[...]
        else:
            xpad[:, 5:8, :] = jnp.zeros((NB, 3, D_MODEL), F32)
            hcar[...] = jnp.zeros((NB, 1, D_MODEL), F32)

    sp = jax.nn.softplus(-lam_ref[...])
    cw = cw_ref[...]
    cb = cb_ref[...]
    rowpos = pos0 + t * T + lax.broadcasted_iota(jnp.int32, (T, 1), 0)

    def one(ib):
        xp = xpad.at[ib]
        x = zx_ref[ib]
        xp[8:8 + T, :] = x
        xc = cb + cw[0:1] * xp[5:5 + T, :]
        xc = xc + cw[1:2] * xp[6:6 + T, :]
        xc = xc + cw[2:3] * xp[7:7 + T, :]
        xc = xc + cw[3:4] * x
        tail = xp[T + 5:T + 8, :]
        xp[5:8, :] = tail
        conv_out_ref[ib] = tail

        xcb = xc.astype(BF16)
        ra = jnp.concatenate(
            [_dot(xcb[:, n * LRU_BW:(n + 1) * LRU_BW], wa_ref[n]) for n in range(LRU_BLOCKS)], axis=1)
        rx = jnp.concatenate(
            [_dot(xcb[:, n * LRU_BW:(n + 1) * LRU_BW], wx_ref[n]) for n in range(LRU_BLOCKS)], axis=1)
        r = jax.nn.sigmoid(ra + ba_ref[...])
        i = jax.nn.sigmoid(rx + bx_ref[...])
        log_a = -LRU_C * r * sp
        a = jnp.exp(log_a)
        th = jnp.tanh(log_a)
        mult = jnp.sqrt(-2.0 * th / (1.0 - th))
        mult = jnp.where(rowpos == 0, 1.0, mult)
        a_scr[...] = a
        u_scr[...] = xc * i * mult

        def step(s, h):
            h = a_scr[pl.ds(s, 1), :] * h + u_scr[pl.ds(s, 1), :]
            h_scr[pl.ds(s, 1), :] = h
            return h

        h_last = lax.fori_loop(0, T, step, hcar[ib], unroll=min(T, 8))
        hcar[ib] = h_last
        lru_out_ref[ib] = h_last
        out_ref[ib] = jax.nn.sigmoid(zm_ref[ib]) * (jax.nn.gelu(zy_ref[ib]) * h_scr[...])

    if NB == 1:
        one(0)
    else:
        def body(ib, c):
            one(ib)
            return c
        lax.fori_loop(0, NB, body, 0)


def _mixa(z3, P, layer, init, T, NB, pos0):
    B, L, _ = z3.shape
    has_init = init is not None

    def zspec(g):
        return pl.BlockSpec((NB, T, D_MODEL), lambda b, t: (b, t, g))

    def pspec(shape):
        nd = len(shape)
        return pl.BlockSpec((None,) + shape, lambda b, t: (layer,) + (0,) * nd)

    in_specs = [zspec(G_AX), zspec(G_AY), zspec(G_MA),
                pspec((CONV_W, D_MODEL)), pspec((1, D_MODEL)),
                pspec((LRU_BLOCKS, LRU_BW, LRU_BW)), pspec((1, D_MODEL)),
                pspec((LRU_BLOCKS, LRU_BW, LRU_BW)), pspec((1, D_MODEL)), pspec((1, D_MODEL))]
    args = [z3, z3, z3, P["conv_w"], P["conv_b"], P["lru_wa"], P["lru_ba"], P["lru_wx"], P["lru_bx"],
            P["lru_lambda"]]
    if has_init:
        conv0, lru0 = init
        in_specs += [pl.BlockSpec((None, NB, CONV_W - 1, D_MODEL), lambda b, t: (layer, b, 0, 0)),
                     pl.BlockSpec((None, NB, 1, D_MODEL), lambda b, t: (layer, b, 0, 0))]
        args += [conv0, lru0]
    out_specs = [pl.BlockSpec((NB, T, D_MODEL), lambda b, t: (b, t, 0)),
                 pl.BlockSpec((NB, CONV_W - 1, D_MODEL), lambda b, t: (b, 0, 0)),
                 pl.BlockSpec((NB, 1, D_MODEL), lambda b, t: (b, 0, 0))]
    out_shape = [jax.ShapeDtypeStruct((B, L, D_MODEL), F32),
                 jax.ShapeDtypeStruct((B, CONV_W - 1, D_MODEL), F32),
                 jax.ShapeDtypeStruct((B, 1, D_MODEL), F32)]
    return pl.pallas_call(
        functools.partial(_mixa_body, T=T, NB=NB, pos0=pos0, has_init=has_init),
        grid=(B // NB, L // T),
        in_specs=in_specs, out_specs=out_specs, out_shape=out_shape,
        scratch_shapes=[pltpu.VMEM((NB, T + 8, D_MODEL), F32), pltpu.VMEM((NB, 1, D_MODEL), F32),
                        pltpu.VMEM((T, D_MODEL), F32), pltpu.VMEM((T, D_MODEL), F32),
                        pltpu.VMEM((T, D_MODEL), F32)],
        compiler_params=_cparams(2),
        name="mix_lru",
    )(*args)


def _mixb_body(*refs, NB, has_init):
    if has_init:
        (zq_ref, zk_ref, zv_ref, zg_ref, zm_ref, cos_ref, sin_ref, d_ref, xi_ref, zeta_ref, gc_ref,
         r0_ref, out_ref, r_out_ref, r_scr) = refs
    else:
        (zq_ref, zk_ref, zv_ref, zg_ref, zm_ref, cos_ref, sin_ref, d_ref, xi_ref, zeta_ref, gc_ref,
         out_ref, r_out_ref, r_scr) = refs
    t = pl.program_id(1)

    @pl.when(t == 0)
    def _():
        if has_init:
            r_scr[...] = r0_ref[...]
        else:
            r_scr[...] = jnp.zeros(r_scr.shape, F32)

    cos = cos_ref[...]
    sin = sin_ref[...]
    half = RET_DK // 2

    def rot(x):
        x1, x2 = x[:, :half], x[:, half:]
        return jnp.concatenate([x1 * cos - x2 * sin, x1 * sin + x2 * cos], axis=1)

    def one(ib):
        rr = r_scr.at[ib]
        for h in range(RET_HEADS):
            sl = slice(h * RET_DK, (h + 1) * RET_DK)
            qr = rot(zq_ref[ib, :, sl])
            kr = rot(zk_ref[ib, :, sl]) * RET_DK ** -0.5
            qb = qr.astype(BF16)
            vb = zv_ref[ib, :, sl].astype(BF16)
            s = _dot_nt(qb, kr.astype(BF16)) * d_ref[h]
            rh = rr[h]
            o = _dot(s.astype(BF16), vb) + _dot(qb, rh.astype(BF16)) * xi_ref[:, sl]
            kz = (kr * zeta_ref[:, sl]).astype(BF16)
            rr[h] = gc_ref[:, sl] * rh + _dot_tn(kz, vb)
            d = o - jnp.mean(o, axis=-1, keepdims=True)
            gn = d * lax.rsqrt(jnp.mean(d * d, axis=-1, keepdims=True) + GN_EPS)
            out_ref[ib, :, sl] = jax.nn.sigmoid(zm_ref[ib, :, sl]) * (gn * jax.nn.silu(zg_ref[ib, :, sl]))
        r_out_ref[ib] = rr[...]

    if NB == 1:
        one(0)
    else:
        def body(ib, c):
            one(ib)
            return c
        lax.fori_loop(0, NB, body, 0)


def _ret_tables(C, L, pos0):
    t = jnp.arange(C, dtype=F32)
    log_g = jnp.log1p(-jnp.exp2(-5.0 - jnp.arange(RET_HEADS, dtype=F32)))
    rel = t[:, None] - t[None, :]
    dmat = jnp.exp(jnp.where(rel[None] >= 0, rel[None] * log_g[:, None, None], -jnp.inf))
    xi = jnp.exp((t[None, :] + 1.0) * log_g[:, None]).T
    zeta = jnp.exp((C - 1.0 - t[None, :]) * log_g[:, None]).T
    g_c = jnp.exp(C * log_g)
    rep = lambda a: jnp.repeat(a, RET_DV, axis=-1)
    half = RET_DK // 2
    pos = pos0 + jnp.arange(L, dtype=jnp.int32)
    inv = ROPE_BASE ** (-jnp.arange(half, dtype=F32) / half)
    ang = pos.astype(F32)[:, None] * inv[None, :]
    return dict(d=dmat, xi=rep(xi), zeta=rep(zeta), gc=rep(g_c[None, :]), cos=jnp.cos(ang), sin=jnp.sin(ang))


def _mixb(z3, tab, layer, init, T, NB):
    B, L, _ = z3.shape
    has_init = init is not None

    def zspec(g):
        return pl.BlockSpec((NB, T, D_MODEL), lambda b, t: (b, t, g))

    in_specs = [zspec(G_RQ), zspec(G_RK), zspec(G_RV), zspec(G_RG), zspec(G_MB),
                pl.BlockSpec((T, RET_DK // 2), lambda b, t: (t, 0)),
                pl.BlockSpec((T, RET_DK // 2), lambda b, t: (t, 0)),
                pl.BlockSpec((RET_HEADS, T, T), lambda b, t: (0, 0, 0)),
                pl.BlockSpec((T, D_MODEL), lambda b, t: (0, 0)),
                pl.BlockSpec((T, D_MODEL), lambda b, t: (0, 0)),
                pl.BlockSpec((1, D_MODEL), lambda b, t: (0, 0))]
    args = [z3, z3, z3, z3, z3, tab["cos"], tab["sin"], tab["d"], tab["xi"], tab["zeta"], tab["gc"]]
    st_shape = (RET_HEADS, RET_DK, RET_DV)
    if has_init:
        in_specs.append(pl.BlockSpec((None, NB) + st_shape, lambda b, t: (layer, b, 0, 0, 0)))
        args.append(init)
    out_specs = [pl.BlockSpec((NB, T, D_MODEL), lambda b, t: (b, t, 0)),
                 pl.BlockSpec((NB,) + st_shape, lambda b, t: (b, 0, 0, 0))]
    out_shape = [jax.ShapeDtypeStruct((B, L, D_MODEL), F32),
                 jax.ShapeDtypeStruct((B,) + st_shape, F32)]
    return pl.pallas_call(
        functools.partial(_mixb_body, NB=NB, has_init=has_init),
        grid=(B // NB, L // T),
        in_specs=in_specs, out_specs=out_specs, out_shape=out_shape,
        scratch_shapes=[pltpu.VMEM((NB,) + st_shape, F32)],
        compiler_params=_cparams(2),
        name="mix_ret",
    )(*args)


def _hg_levels(C):
    ms = []
    m = C // 2
    while m >= 1:
        ms.append(m)
        m //= 2
    return ms


def _hg_tables(C):
    t = np.arange(C)
    masks = []
    for m in _hg_levels(C):
        blk = 2 * m
        same = (t[:, None] // blk) == (t[None, :] // blk)
        masks.append(same & ((t[:, None] % blk) >= m) & ((t[None, :] % blk) < m))
    masks.append(np.eye(C, dtype=bool))
    ltri = np.tril(np.ones((C, C), np.float32))
    return dict(masks=jnp.asarray(np.stack(masks).astype(np.float32)), ltri=jnp.asarray(ltri))


def _mixc_body(*refs, T, C, NB, layer, has_init):
    if has_init:
        (zq_ref, zf_ref, zi_ref, zg_ref, zm_ref, lb_ref, nw_ref, ltri_ref, mask_ref,
         s0_ref, out_ref, s_out_ref, st_scr, b_scr) = refs
    else:
        (zq_ref, zf_ref, zi_ref, zg_ref, zm_ref, lb_ref, nw_ref, ltri_ref, mask_ref,
         out_ref, s_out_ref, st_scr, b_scr) = refs
    t = pl.program_id(1)
    nt = pl.num_programs(1)

    @pl.when(t == 0)
    def _():
        if has_init:
            for h in range(HG_HEADS):
                for ib in range(NB):
                    st_scr[ib, h] = s0_ref[ib, h].T
        else:
            st_scr[...] = jnp.zeros(st_scr.shape, F32)

    if layer > 0:
        x = lb_ref[...]
        e = jnp.exp(x - jnp.max(x, axis=0, keepdims=True))
        sm = e / jnp.sum(e, axis=0, keepdims=True)
        lb = sm[1:2]
        for j in range(2, layer + 1):
            lb = lb + sm[j:j + 1]
    levels = _hg_levels(C)
    gs = min(8, C)
    rowi = lax.broadcasted_iota(jnp.int32, (gs, HG_DK), 0)

    def bref_level(m, sl):
        blk = 2 * m
        if blk >= 8:
            pieces = [jnp.broadcast_to(b_scr[j * blk + m - 1:j * blk + m, sl], (blk, HG_DK))
                      for j in range(C // blk)]
        else:
            pieces = []
            for g in range(C // gs):
                acc = None
                for j in range(gs // blk):
                    r = g * gs + j * blk + m - 1
                    bc = jnp.broadcast_to(b_scr[r:r + 1, sl], (gs, HG_DK))
                    acc = bc if acc is None else jnp.where(rowi >= j * blk, bc, acc)
                pieces.append(acc)
        return pieces[0] if len(pieces) == 1 else jnp.concatenate(pieces, axis=0)

    def chunk(ib, r0):
        rows = pl.ds(r0, C)
        for h in range(HG_HEADS):
            sl = slice(h * HG_DK, (h + 1) * HG_DK)
            cf = zf_ref[ib, rows, sl]
            if layer == 0:
                logf = jax.nn.log_sigmoid(cf)
                kk = jax.nn.sigmoid(-cf)
            else:
                lbh = lb[:, sl]
                logf = jnp.log(lbh + (1.0 - lbh) * jax.nn.sigmoid(cf))
                kk = (1.0 - lbh) * jax.nn.sigmoid(-cf)
            qq = jax.nn.silu(zq_ref[ib, rows, sl])
            vb = zi_ref[ib, rows, sl].astype(BF16)
            b = jnp.dot(ltri_ref[...], logf, precision=lax.Precision.HIGHEST,
                        preferred_element_type=F32)
            b_scr[:, sl] = b
            bend = b[C - 1:C, :]
            qe = (qq * jnp.exp(b)).astype(BF16)
            ke = (kk * jnp.exp(bend - b)).astype(BF16)
            amat = mask_ref[len(levels)] * _dot_nt(qq.astype(BF16), kk.astype(BF16))
            for li, m in enumerate(levels):
                e = jnp.exp(-jnp.abs(b - bref_level(m, sl)))
                amat = amat + mask_ref[li] * _dot_nt((qq * e).astype(BF16), (kk * e).astype(BF16))
            sth = st_scr[ib, h]
            o = _dot_nt(qe, sth.astype(BF16)) + _dot(amat.astype(BF16), vb)
            st_scr[ib, h] = jnp.exp(bend) * sth + _dot_tn(vb, ke)
            rms = o * lax.rsqrt(jnp.mean(o * o, axis=-1, keepdims=True) + EPS)
            out_ref[ib, rows, sl] = jax.nn.sigmoid(zm_ref[ib, rows, sl]) * (
                rms * nw_ref[:, sl] * jax.nn.silu(zg_ref[ib, rows, sl]))

    def one(ib):
        if T == C:
            chunk(ib, 0)
        else:
            def cbody(c, carry):
                chunk(ib, pl.multiple_of(c * C, C))
                return carry
            lax.fori_loop(0, T // C, cbody, 0)

        @pl.when(t == nt - 1)
        def _():
            for h in range(HG_HEADS):
                s_out_ref[ib, h] = st_scr[ib, h].T

    if NB == 1:
        one(0)
    else:
        def body(ib, c):
            one(ib)
            return c
        lax.fori_loop(0, NB, body, 0)


def _mixc(z3, P, tab, layer, init, T, C, NB):
    B, L, _ = z3.shape
    has_init = init is not None
    nlev = len(_hg_levels(C)) + 1

    def zspec(g):
        return pl.BlockSpec((NB, T, D_MODEL), lambda b, t: (b, t, g))

    in_specs = [zspec(G_CQ), zspec(G_CF), zspec(G_CI), zspec(G_CG), zspec(G_MC),
                pl.BlockSpec((DEPTH, D_MODEL), lambda b, t: (0, 0)),
                pl.BlockSpec((None, 1, D_MODEL), lambda b, t: (layer, 0, 0)),
                pl.BlockSpec((C, C), lambda b, t: (0, 0)),
                pl.BlockSpec((nlev, C, C), lambda b, t: (0, 0, 0))]
    args = [z3, z3, z3, z3, z3, P["hg_lb"], P["hg_norm_w"], tab["ltri"], tab["masks"]]
    st_shape = (HG_HEADS, HG_DK, HG_DV)
    if has_init:
        in_specs.append(pl.BlockSpec((None, NB) + st_shape, lambda b, t: (layer, b, 0, 0, 0)))
        args.append(init)
    out_specs = [pl.BlockSpec((NB, T, D_MODEL), lambda b, t: (b, t, 0)),
                 pl.BlockSpec((NB,) + st_shape, lambda b, t: (b, 0, 0, 0))]
    out_shape = [jax.ShapeDtypeStruct((B, L, D_MODEL), F32),
                 jax.ShapeDtypeStruct((B,) + st_shape, F32)]
    return pl.pallas_call(
        functools.partial(_mixc_body, T=T, C=C, NB=NB, layer=layer, has_init=has_init),
        grid=(B // NB, L // T),
        in_specs=in_specs, out_specs=out_specs, out_shape=out_shape,
        scratch_shapes=[pltpu.VMEM((NB, HG_HEADS, HG_DV, HG_DK), F32), pltpu.VMEM((C, D_MODEL), F32)],
        compiler_params=_cparams(2),
        name="mix_hgrn",
    )(*args)


def _post_body(*refs, final):
    if final:
        (h_ref, a_ref, b_ref, c_ref, wo_ref, nw_ref, wg_ref, wu_ref, wd_ref, nf_ref, hout_ref, y_ref) = refs
    else:
        (h_ref, a_ref, b_ref, c_ref, wo_ref, nw_ref, wg_ref, wu_ref, wd_ref, hout_ref) = refs
    mix = (a_ref[...] + b_ref[...]) + c_ref[...]
    h1 = h_ref[...] + _dot(mix.astype(BF16), wo_ref[...])
    u = _rms(h1, nw_ref[...]).astype(BF16)
    act = (jax.nn.silu(_dot(u, wg_ref[...])) * _dot(u, wu_ref[...])).astype(BF16)
    h2 = h1 + _dot(act, wd_ref[...])
    hout_ref[...] = h2
    if final:
        y_ref[...] = _rms(h2, nf_ref[...])


def _post(h, oa, ob, oc, P, layer, tm, final):
    n = h.shape[0]
    row = pl.BlockSpec((tm, D_MODEL), lambda i: (i, 0))

    def wspec(shape):
        return pl.BlockSpec((None,) + shape, lambda i: (layer, 0, 0), pipeline_mode=pl.Buffered(1))

    in_specs = [row, row, row, row,
                wspec((D_MODEL, D_MODEL)), wspec((1, D_MODEL)),
                wspec((D_MODEL, D_FF)), wspec((D_MODEL, D_FF)), wspec((D_FF, D_MODEL))]
    args = [h, oa, ob, oc, P["w_out"], P["norm_ffn_w"], P["w_gate"], P["w_up"], P["w_down"]]
    out_specs = [row]
    out_shape = [jax.ShapeDtypeStruct((n, D_MODEL), F32)]
    if final:
        in_specs.append(pl.BlockSpec((1, D_MODEL), lambda i: (0, 0)))
        args.append(P["norm_final_w"])
        out_specs.append(row)
        out_shape.append(jax.ShapeDtypeStruct((n, D_MODEL), F32))
    return pl.pallas_call(
        functools.partial(_post_body, final=final),
        grid=(n // tm,),
        in_specs=in_specs, out_specs=out_specs, out_shape=out_shape,
        compiler_params=_cparams(1),
        name="post",
    )(*args)


def _trunk_cfg(B, L):
    n = B * L
    if L >= 256:
        return dict(tm=min(n, 1024), NB=1, TA=256, TB=256, TC=256, CC=64, tp=min(n, 512))
    return dict(tm=min(n, 512), NB=min(B, 8), TA=L, TB=L, TC=L, CC=L, tp=min(n, 512))


def _run_trunk(x, pos0, init, P):
    B, L, _ = x.shape
    cfg = _trunk_cfg(B, L)
    n = B * L
    h = x.reshape(n, D_MODEL).astype(F32)
    rtab = _ret_tables(cfg["TB"], L, pos0)
    htab = _hg_tables(cfg["CC"])
    convs, lrus, rets, hgs = [], [], [], []
    y = None
    for l in range(DEPTH):
        z = _proj(h, P["norm_mix_w"], P["w_in"], l, cfg["tm"], D_MODEL)
        z3 = z.reshape(B, L, IN_COLS)
        ia = None if init is None else (init[0], init[1])
        oa, conv_l, lru_l = _mixa(z3, P, l, ia, cfg["TA"], cfg["NB"], pos0)
        ob, ret_l = _mixb(z3, rtab, l, None if init is None else init[2], cfg["TB"], cfg["NB"])
        oc, hg_l = _mixc(z3, P, htab, l, None if init is None else init[3], cfg["TC"], cfg["CC"], cfg["NB"])
        res = _post(h, oa.reshape(n, D_MODEL), ob.reshape(n, D_MODEL), oc.reshape(n, D_MODEL), P, l,
                    cfg["tp"], l == DEPTH - 1)
        h = res[0]
        if l == DEPTH - 1:
            y = res[1]
        convs.append(conv_l)
        lrus.append(lru_l.reshape(B, D_MODEL))
        rets.append(ret_l)
        hgs.append(hg_l)
    return (y.reshape(B, L, D_MODEL), jnp.stack(convs), jnp.stack(lrus), jnp.stack(rets), jnp.stack(hgs))


def kernel(x_prompt, x_sample, state_conv, state_lru, state_ret, state_hgrn, norm_mix_w, w_in, conv_w, conv_b, lru_wa, lru_ba, lru_wx, lru_bx, lru_lambda, hg_lb, hg_norm_w, w_out, norm_ffn_w, w_gate, w_up, w_down, norm_final_w):
    row = lambda a: a.astype(F32).reshape(DEPTH, 1, D_MODEL)
    P = dict(
        norm_mix_w=row(norm_mix_w), w_in=w_in.astype(BF16),
        conv_w=conv_w.astype(F32), conv_b=row(conv_b),
        lru_wa=lru_wa.astype(BF16), lru_ba=row(lru_ba), lru_wx=lru_wx.astype(BF16), lru_bx=row(lru_bx),
        lru_lambda=row(lru_lambda), hg_lb=hg_lb.astype(F32), hg_norm_w=row(hg_norm_w),
        w_out=w_out.astype(BF16), norm_ffn_w=row(norm_ffn_w),
        w_gate=w_gate.astype(BF16), w_up=w_up.astype(BF16), w_down=w_down.astype(BF16),
        norm_final_w=norm_final_w.astype(F32).reshape(1, D_MODEL))
    init_s = (state_conv.astype(F32), state_lru.astype(F32).reshape(DEPTH, -1, 1, D_MODEL),
              state_ret.astype(F32), state_hgrn.astype(F32))
    y_p, conv_p, lru_p, ret_p, hg_p = _run_trunk(x_prompt, 0, None, P)
    y_s, conv_s, lru_s, ret_s, hg_s = _run_trunk(x_sample, PAST_LEN, init_s, P)
    return (y_p.astype(x_prompt.dtype), y_s.astype(x_sample.dtype),
            conv_p.astype(state_conv.dtype), lru_p.astype(state_lru.dtype),
            ret_p.astype(state_ret.dtype), hg_p.astype(state_hgrn.dtype),
            conv_s.astype(state_conv.dtype), lru_s.astype(state_lru.dtype),
            ret_s.astype(state_ret.dtype), hg_s.astype(state_hgrn.dtype))
```

```python
import functools

import numpy as np
import jax
import jax.numpy as jnp
from jax import lax
from jax.experimental import pallas as pl
from jax.experimental.pallas import tpu as pltpu

F32 = jnp.float32
BF16 = jnp.bfloat16

D_MODEL = 1024
DEPTH = 4
PAST_LEN = 16384
LRU_BLOCKS = 4
LRU_BW = D_MODEL // LRU_BLOCKS
CONV_W = 4
LRU_C = 8.0
RET_HEADS = 4
RET_DK = 256
RET_DV = 256
ROPE_BASE = 10000.0
HG_DK = 128
HG_HEADS = 8
HG_DV = 128
D_FF = 2816
EPS = 1e-6
GN_EPS = 1e-5
LOG2E = 1.4426950408889634
N_GROUPS = 13
IN_COLS = N_GROUPS * D_MODEL
G_AX, G_AY, G_RQ, G_RK, G_RV, G_RG, G_CQ, G_CF, G_CI, G_CG, G_MA, G_MB, G_MC = range(13)

VMEM_LIMIT = 56 * 1024 * 1024
SUBLANES = 8
LANES = 128


def _cparams(n_axes):
    return pltpu.CompilerParams(dimension_semantics=("arbitrary",) * n_axes,
                                vmem_limit_bytes=VMEM_LIMIT)


def _rms(x, w):
    return x * lax.rsqrt(jnp.mean(x * x, axis=-1, keepdims=True) + EPS) * w


def _dot(a, b):
    return jnp.dot(a, b, preferred_element_type=F32)


def _dot_nt(a, b):
    return lax.dot_general(a, b, (((1,), (1,)), ((), ())), preferred_element_type=F32)


def _dot_tn(a, b):
    return lax.dot_general(a, b, (((0,), (0,)), ((), ())), preferred_element_type=F32)


def _stacked_state(prev, shape_tail, nb, layer, batch):
    nd = len(shape_tail)
    out_spec = pl.BlockSpec((None, nb) + shape_tail, lambda b, t: (layer, b) + (0,) * nd)
    out_shape = jax.ShapeDtypeStruct((DEPTH, batch) + shape_tail, F32)
    if prev is None:
        return [], [], out_spec, out_shape
    return [pl.BlockSpec(memory_space=pl.ANY)], [prev], out_spec, out_shape


def _proj_body(h_ref, nw_ref, w_ref, z_ref, u_scr):
    @pl.when(pl.program_id(1) == 0)
    def _():
        u_scr[...] = _rms(h_ref[...], nw_ref[...]).astype(BF16)

    z_ref[...] = _dot(u_scr[...], w_ref[...])


def _proj(h, nw_all, w_all, layer, tm, tn):
    n = h.shape[0]
    ncols = w_all.shape[2]
    return pl.pallas_call(
        _proj_body,
        grid=(n // tm, ncols // tn),
        in_specs=[
            pl.BlockSpec((tm, D_MODEL), lambda i, j: (i, 0)),
            pl.BlockSpec((None, 1, D_MODEL), lambda i, j: (layer, 0, 0)),
            pl.BlockSpec((None, D_MODEL, tn), lambda i, j: (layer, 0, j)),
        ],
        out_specs=pl.BlockSpec((tm, tn), lambda i, j: (i, j)),
        out_shape=jax.ShapeDtypeStruct((n, ncols), F32),
        scratch_shapes=[pltpu.VMEM((tm, D_MODEL), BF16)],
        compiler_params=_cparams(2),
        name="proj",
    )(h, nw_all, w_all)


def _mixa_body(*refs, T, NB, pos0, has_init, n_prev):
    (zx_ref, zy_ref, zm_ref, cw_ref, cb_ref, wa_ref, ba_ref, wx_ref, bx_ref, lam_ref) = refs[:10]
    k = 10
    if has_init:
        conv0_ref, lru0_ref = refs[k:k + 2]
        k += 2
    k += n_prev
    out_ref, conv_out_ref, lru_out_ref, xpad, hcar, a_scr, u_scr, h_scr = refs[k:]
    t = pl.program_id(1)

    @pl.when(t == 0)
    def _():
        if has_init:
            xpad[:, 5:8, :] = conv0_ref[...]
            hcar[...] = lru0_ref[...]
        else:
            xpad[:, 5:8, :] = jnp.zeros((NB, 3, D_MODEL), F32)
            hcar[...] = jnp.zeros((NB, 1, D_MODEL), F32)

    sp = jax.nn.softplus(-lam_ref[...])
    cw = cw_ref[...]
    cb = cb_ref[...]
    rowpos = pos0 + t * T + lax.broadcasted_iota(jnp.int32, (T, 1), 0)

    def one(ib):
        xp = xpad.at[ib]
        x = zx_ref[ib]
        xp[8:8 + T, :] = x
        xc = cb + cw[0:1] * xp[5:5 + T, :]
        xc = xc + cw[1:2] * xp[6:6 + T, :]
        xc = xc + cw[2:3] * xp[7:7 + T, :]
        xc = xc + cw[3:4] * x
        tail = xp[T + 5:T + 8, :]
        xp[5:8, :] = tail
        conv_out_ref[ib] = tail

        xcb = xc.astype(BF16)
        ra = jnp.concatenate(
            [_dot(xcb[:, n * LRU_BW:(n + 1) * LRU_BW], wa_ref[n]) for n in range(LRU_BLOCKS)], axis=1)
        rx = jnp.concatenate(
            [_dot(xcb[:, n * LRU_BW:(n + 1) * LRU_BW], wx_ref[n]) for n in range(LRU_BLOCKS)], axis=1)
        r = jax.nn.sigmoid(ra + ba_ref[...])
        i = jax.nn.sigmoid(rx + bx_ref[...])
        log_a = -LRU_C * r * sp
        a = jnp.exp(log_a)
        th = jnp.tanh(log_a)
        mult = jnp.sqrt(-2.0 * th / (1.0 - th))
        mult = jnp.where(rowpos == 0, 1.0, mult)
        a_scr[...] = a
        u_scr[...] = xc * i * mult

        def step(s, h):
            h = a_scr[pl.ds(s, 1), :] * h + u_scr[pl.ds(s, 1), :]
            h_scr[pl.ds(s, 1), :] = h
            return h

        h_last = lax.fori_loop(0, T, step, hcar[ib], unroll=min(T, 8))
        hcar[ib] = h_last
        lru_out_ref[ib] = h_last
        out_ref[ib] = jax.nn.sigmoid(zm_ref[ib]) * (jax.nn.gelu(zy_ref[ib]) * h_scr[...])

    if NB == 1:
        one(0)
    else:
        def body(ib, c):
            one(ib)
            return c
        lax.fori_loop(0, NB, body, 0)


def _mixa(z3, P, layer, init, prev, T, NB, pos0):
    B, L, _ = z3.shape
    has_init = init is not None

    def zspec(g):
        return pl.BlockSpec((NB, T, D_MODEL), lambda b, t: (b, t, g))

    def pspec(shape):
        nd = len(shape)
        return pl.BlockSpec((None,) + shape, lambda b, t: (layer,) + (0,) * nd)

    in_specs = [zspec(G_AX), zspec(G_AY), zspec(G_MA),
                pspec((CONV_W, D_MODEL)), pspec((1, D_MODEL)),
                pspec((LRU_BLOCKS, LRU_BW, LRU_BW)), pspec((1, D_MODEL)),
                pspec((LRU_BLOCKS, LRU_BW, LRU_BW)), pspec((1, D_MODEL)), pspec((1, D_MODEL))]
    args = [z3, z3, z3, P["conv_w"], P["conv_b"], P["lru_wa"], P["lru_ba"], P["lru_wx"], P["lru_bx"],
            P["lru_lambda"]]
    if has_init:
        conv0, lru0 = init
        in_specs += [pl.BlockSpec((None, NB, CONV_W - 1, D_MODEL), lambda b, t: (layer, b, 0, 0)),
                     pl.BlockSpec((None, NB, 1, D_MODEL), lambda b, t: (layer, b, 0, 0))]
        args += [conv0, lru0]
    pc, pl_ = (None, None) if prev is None else prev
    xs1, xa1, conv_spec, conv_shape = _stacked_state(pc, (CONV_W - 1, D_MODEL), NB, layer, B)
    xs2, xa2, lru_spec, lru_shape = _stacked_state(pl_, (1, D_MODEL), NB, layer, B)
    aliases = {} if prev is None else {len(args): 1, len(args) + 1: 2}
    in_specs += xs1 + xs2
    args += xa1 + xa2
    out_specs = [pl.BlockSpec((NB, T, D_MODEL), lambda b, t: (b, t, 0)), conv_spec, lru_spec]
    out_shape = [jax.ShapeDtypeStruct((B, L, D_MODEL), F32), conv_shape, lru_shape]
    return pl.pallas_call(
        functools.partial(_mixa_body, T=T, NB=NB, pos0=pos0, has_init=has_init, n_prev=len(xa1 + xa2)),
        grid=(B // NB, L // T),
        in_specs=in_specs, out_specs=out_specs, out_shape=out_shape,
        input_output_aliases=aliases,
        scratch_shapes=[pltpu.VMEM((NB, T + 8, D_MODEL), F32), pltpu.VMEM((NB, 1, D_MODEL), F32),
                        pltpu.VMEM((T, D_MODEL), F32), pltpu.VMEM((T, D_MODEL), F32),
                        pltpu.VMEM((T, D_MODEL), F32)],
        compiler_params=_cparams(2),
        name="mix_lru",
    )(*args)


def _mixb_body(*refs, NB, has_init, n_prev):
    (zq_ref, zk_ref, zv_ref, zg_ref, zm_ref, cos_ref, sin_ref, d_ref, xi_ref, zeta_ref, gc_ref) = refs[:11]
    k = 11
    if has_init:
        r0_ref = refs[k]
        k += 1
    k += n_prev
    out_ref, r_out_ref, r_scr = refs[k:]
    t = pl.program_id(1)

    @pl.when(t == 0)
    def _():
        if has_init:
            r_scr[...] = r0_ref[...]
        else:
            r_scr[...] = jnp.zeros(r_scr.shape, F32)

    cos = cos_ref[...]
    sin = sin_ref[...]
    half = RET_DK // 2

    def rot(x):
        x1, x2 = x[:, :half], x[:, half:]
        return jnp.concatenate([x1 * cos - x2 * sin, x1 * sin + x2 * cos], axis=1)

    def one(ib):
        rr = r_scr.at[ib]
        for h in range(RET_HEADS):
            sl = slice(h * RET_DK, (h + 1) * RET_DK)
            qr = rot(zq_ref[ib, :, sl])
            kr = rot(zk_ref[ib, :, sl]) * RET_DK ** -0.5
            qb = qr.astype(BF16)
            vb = zv_ref[ib, :, sl].astype(BF16)
            s = _dot_nt(qb, kr.astype(BF16)) * d_ref[h]
            rh = rr[h]
            o = _dot(s.astype(BF16), vb) + _dot(qb, rh.astype(BF16)) * xi_ref[:, sl]
            kz = (kr * zeta_ref[:, sl]).astype(BF16)
            rr[h] = gc_ref[:, sl] * rh + _dot_tn(kz, vb)
            d = o - jnp.mean(o, axis=-1, keepdims=True)
            gn = d * lax.rsqrt(jnp.mean(d * d, axis=-1, keepdims=True) + GN_EPS)
            out_ref[ib, :, sl] = jax.nn.sigmoid(zm_ref[ib, :, sl]) * (gn * jax.nn.silu(zg_ref[ib, :, sl]))
        r_out_ref[ib] = rr[...]

    if NB == 1:
        one(0)
    else:
        def body(ib, c):
            one(ib)
            return c
        lax.fori_loop(0, NB, body, 0)


def _ret_tables(C, L, pos0):
    t = jnp.arange(C, dtype=F32)
    log_g = jnp.log1p(-jnp.exp2(-5.0 - jnp.arange(RET_HEADS, dtype=F32)))
    rel = t[:, None] - t[None, :]
    dmat = jnp.exp(jnp.where(rel[None] >= 0, rel[None] * log_g[:, None, None], -jnp.inf))
    xi = jnp.exp((t[None, :] + 1.0) * log_g[:, None]).T
    zeta = jnp.exp((C - 1.0 - t[None, :]) * log_g[:, None]).T
    g_c = jnp.exp(C * log_g)
    rep = lambda a: jnp.repeat(a, RET_DV, axis=-1)
    half = RET_DK // 2
    pos = pos0 + jnp.arange(L, dtype=jnp.int32)
    inv = ROPE_BASE ** (-jnp.arange(half, dtype=F32) / half)
    ang = pos.astype(F32)[:, None] * inv[None, :]
    return dict(d=dmat, xi=rep(xi), zeta=rep(zeta), gc=rep(g_c[None, :]), cos=jnp.cos(ang), sin=jnp.sin(ang))


def _mixb(z3, tab, layer, init, prev, T, NB):
    B, L, _ = z3.shape
    has_init = init is not None

    def zspec(g):
        return pl.BlockSpec((NB, T, D_MODEL), lambda b, t: (b, t, g))

    in_specs = [zspec(G_RQ), zspec(G_RK), zspec(G_RV), zspec(G_RG), zspec(G_MB),
                pl.BlockSpec((T, RET_DK // 2), lambda b, t: (t, 0)),
                pl.BlockSpec((T, RET_DK // 2), lambda b, t: (t, 0)),
                pl.BlockSpec((RET_HEADS, T, T), lambda b, t: (0, 0, 0)),
                pl.BlockSpec((T, D_MODEL), lambda b, t: (0, 0)),
                pl.BlockSpec((T, D_MODEL), lambda b, t: (0, 0)),
                pl.BlockSpec((1, D_MODEL), lambda b, t: (0, 0))]
    args = [z3, z3, z3, z3, z3, tab["cos"], tab["sin"], tab["d"], tab["xi"], tab["zeta"], tab["gc"]]
    st_shape = (RET_HEADS, RET_DK, RET_DV)
    if has_init:
        in_specs.append(pl.BlockSpec((None, NB) + st_shape, lambda b, t: (layer, b, 0, 0, 0)))
        args.append(init)
    xs, xa, st_spec, st_out = _stacked_state(prev, st_shape, NB, layer, B)
    aliases = {} if prev is None else {len(args): 1}
    in_specs += xs
    args += xa
    out_specs = [pl.BlockSpec((NB, T, D_MODEL), lambda b, t: (b, t, 0)), st_spec]
    out_shape = [jax.ShapeDtypeStruct((B, L, D_MODEL), F32), st_out]
    return pl.pallas_call(
        functools.partial(_mixb_body, NB=NB, has_init=has_init, n_prev=len(xa)),
        grid=(B // NB, L // T),
        in_specs=in_specs, out_specs=out_specs, out_shape=out_shape,
        input_output_aliases=aliases,
        scratch_shapes=[pltpu.VMEM((NB,) + st_shape, F32)],
        compiler_params=_cparams(2),
        name="mix_ret",
    )(*args)


def _hg_levels(lseg):
    ms = []
    m = lseg // 2
    while m >= 1:
        ms.append(m)
        m //= 2
    return ms


def _hg_tables(C, lseg):
    t = np.arange(C)
    masks, sgn = [], []
    for m in _hg_levels(lseg):
        blk = 2 * m
        same = (t[:, None] // blk) == (t[None, :] // blk)
        masks.append(same & ((t[:, None] % blk) >= m) & ((t[None, :] % blk) < m))
        if m < SUBLANES:
            sgn.append(np.where((t % blk) >= m, 1.0, -1.0)[:, None] * np.ones((1, LANES)))
    masks.append(np.eye(C, dtype=bool))
    masks = np.stack(masks).astype(np.float32)
    masks2 = np.concatenate([masks, masks], axis=2)
    seg = t // lseg
    ltri = ((seg[:, None] == seg[None, :]) & (t[None, :] <= t[:, None])).astype(np.float32)
    sgn = np.stack(sgn).astype(np.float32) if sgn else np.ones((1, C, LANES), np.float32)
    return dict(masks=jnp.asarray(masks2), ltri=jnp.asarray(ltri), sgn=jnp.asarray(sgn))


def _row_bcast(ref, C, blk, off):
    W = ref.shape[1]
    if blk >= SUBLANES:
        pieces = [jnp.broadcast_to(ref[j * blk + off:j * blk + off + 1, :], (blk, W)) for j in range(C // blk)]
    else:
        rowi = lax.broadcasted_iota(jnp.int32, (SUBLANES, W), 0)
        pieces = []
        for g in range(C // SUBLANES):
            acc = None
            for j in range(SUBLANES // blk):
                r = g * SUBLANES + j * blk + off
                bc = jnp.broadcast_to(ref[r:r + 1, :], (SUBLANES, W))
                acc = bc if acc is None else jnp.where(rowi >= j * blk, bc, acc)
            pieces.append(acc)
    return pieces[0] if len(pieces) == 1 else jnp.concatenate(pieces, axis=0)


def _mixc_body(*refs, RT, C, lseg, layer, has_init, n_prev):
    (zq_ref, zf_ref, zi_ref, zg_ref, zm_ref, lb_ref, nw_ref, ltri_ref, mask_ref, sgn_ref) = refs[:10]
    k = 10
    if has_init:
        s0_ref = refs[k]
        k += 1
    k += n_prev
    out_ref, s_out_ref, st_scr, b_scr, qe_scr, ke_scr, oi_scr, bh_scr = refs[k:]
    t = pl.program_id(1)
    NS = C // lseg
    NP = HG_HEADS // 2
    PW = 2 * HG_DK

    @pl.when(t == 0)
    def _():
        if has_init:
            st_scr[...] = s0_ref[...]
        else:
            st_scr[...] = jnp.zeros(st_scr.shape, F32)

    if layer > 0:
        x = lb_ref[...]
        e = jnp.exp(x - jnp.max(x, axis=0, keepdims=True))
        sm = e / jnp.sum(e, axis=0, keepdims=True)
        lb = sm[1:2]
        for j in range(2, layer + 1):
            lb = lb + sm[j:j + 1]
    levels = _hg_levels(lseg)

    def blockdiag_rows(pair):
        z = jnp.zeros((pair.shape[0], HG_DK), pair.dtype)
        return jnp.concatenate([jnp.concatenate([pair[:, :HG_DK], z], axis=1),
                                jnp.concatenate([z, pair[:, HG_DK:]], axis=1)], axis=0)

    def chunk(r0):
        rows = pl.ds(r0, C)
        cf = zf_ref[rows, :]
        if layer == 0:
            logf = jax.nn.log_sigmoid(cf)
            kk = jax.nn.sigmoid(-cf)
        else:
            sg = jax.nn.sigmoid(cf)
            logf = jnp.log(lb + (1.0 - lb) * sg)
            kk = (1.0 - lb) * (1.0 - sg)
        qq = jax.nn.silu(zq_ref[rows, :])
        vb = zi_ref[rows, :].astype(BF16)
        b2 = jnp.dot(ltri_ref[...], logf, precision=lax.Precision.HIGHEST, preferred_element_type=F32) * LOG2E
        b_scr[...] = b2
        bend = _row_bcast(b_scr, C, lseg, lseg - 1)
        qe = qq * jnp.exp2(b2)
        ke = kk * jnp.exp2(bend - b2)
        qb = qq.astype(BF16)
        kb = kk.astype(BF16)

        def level_w(li, m):
            blk = 2 * m
            if m >= SUBLANES:
                pieces = []
                for j in range(C // blk):
                    lo = slice(j * blk, j * blk + m)
                    up = slice(j * blk + m, (j + 1) * blk)
                    brow = b_scr[j * blk + m - 1:j * blk + m, :]
                    pieces.append(kk[lo] * jnp.exp2(brow - b2[lo]))
                    pieces.append(qq[up] * jnp.exp2(b2[up] - brow))
                w = jnp.concatenate(pieces, axis=0)
            else:
                si = li - (len(levels) - sgn_ref.shape[0])
                sgn = jnp.concatenate([sgn_ref[si]] * (D_MODEL // LANES), axis=1)
                e = jnp.exp2((b2 - _row_bcast(b_scr, C, blk, m - 1)) * sgn)
                w = jnp.where(sgn > 0.0, qq, kk) * e
            return w.astype(BF16)

        amat = [mask_ref[len(levels)] * _dot_nt(qb[:, p * PW:(p + 1) * PW], blockdiag_rows(kb[:, p * PW:(p + 1) * PW]))
                for p in range(NP)]
        for li, m in enumerate(levels):
            w = level_w(li, m)
            for p in range(NP):
                wp = w[:, p * PW:(p + 1) * PW]
                amat[p] = amat[p] + mask_ref[li] * _dot_nt(wp, blockdiag_rows(wp))
        o_intra = jnp.concatenate(
            [_dot(amat[p].astype(BF16), blockdiag_rows(vb[:, p * PW:(p + 1) * PW])) for p in range(NP)], axis=1)

        if NS == 1:
            bcol = b_scr[C - SUBLANES:C, :]
            o_parts = []
            for p in range(NP):
                sbd = blockdiag_rows(jnp.concatenate([st_scr[0, 2 * p], st_scr[0, 2 * p + 1]], axis=1))
                o_parts.append(_dot(qe[:, p * PW:(p + 1) * PW].astype(BF16), sbd.astype(BF16)))
            o_inter = jnp.concatenate(o_parts, axis=1)
            keb = ke.astype(BF16)
            for h in range(HG_HEADS):
                sl = slice(h * HG_DK, (h + 1) * HG_DK)
                dec = jnp.exp2(bcol[:, sl].T[:, SUBLANES - 1:SUBLANES])
                st_scr[0, h] = dec * st_scr[0, h] + _dot_tn(keb[:, sl], vb[:, sl])
        else:
            qe_scr[...] = qe
            ke_scr[...] = ke
            for h in range(HG_HEADS):
                sl = slice(h * HG_DK, (h + 1) * HG_DK)
                bh_scr[h] = b2[:, sl]
                bends = bh_scr[h, pl.ds(lseg - 1, NS, stride=lseg), :]
                dec_all = jnp.exp2(bends.T)
                for s in range(NS):
                    rs = slice(s * lseg, (s + 1) * lseg)
                    st = st_scr[s, h]
                    oi_scr[rs, sl] = _dot(qe_scr[rs, sl].astype(BF16), st.astype(BF16))
                    st_scr[s, h] = dec_all[:, s:s + 1] * st + _dot_tn(
                        ke_scr[rs, sl].astype(BF16), zi_ref[pl.ds(r0 + s * lseg, lseg), sl].astype(BF16))
            o_inter = oi_scr[...]

        o = o_inter + o_intra
        gate = jax.nn.sigmoid(zm_ref[rows, :])
        og = nw_ref[...] * jax.nn.silu(zg_ref[rows, :])
        for h in range(HG_HEADS):
            sl = slice(h * HG_DK, (h + 1) * HG_DK)
            oh = o[:, sl]
            rms = oh * lax.rsqrt(jnp.mean(oh * oh, axis=-1, keepdims=True) + EPS)
            out_ref[rows, sl] = gate[:, sl] * (rms * og[:, sl])

    if RT == C:
        chunk(0)
    else:
        def cbody(c, carry):
            chunk(pl.multiple_of(c * C, C))
            return carry
        lax.fori_loop(0, RT // C, cbody, 0)

    s_out_ref[...] = st_scr[...]


def _mixc(z, P, tab, layer, init, prev, B, L, RT, C, lseg):
    n = B * L
    has_init = init is not None
    NS = C // lseg
    nt = L // RT if lseg == C else 1
    nlev = len(_hg_levels(lseg)) + 1

    def zspec(g):
        return pl.BlockSpec((RT, D_MODEL), lambda b, t: (b * nt + t, g))

    const = lambda shape: pl.BlockSpec(shape, lambda b, t: (0,) * len(shape))
    in_specs = [zspec(G_CQ), zspec(G_CF), zspec(G_CI), zspec(G_CG), zspec(G_MC),
                const((DEPTH, D_MODEL)),
                pl.BlockSpec((None, 1, D_MODEL), lambda b, t: (layer, 0, 0)),
                const((C, C)), const((nlev, C, 2 * C)), const(tab["sgn"].shape)]
    args = [z, z, z, z, z, P["hg_lb"], P["hg_norm_w"], tab["ltri"], tab["masks"], tab["sgn"]]
    st_shape = (HG_HEADS, HG_DK, HG_DV)
    if has_init:
        in_specs.append(pl.BlockSpec((None, NS) + st_shape, lambda b, t: (layer, b, 0, 0, 0)))
        args.append(init)
    xs, xa, st_spec, st_out = _stacked_state(prev, st_shape, NS, layer, B)
    aliases = {} if prev is None else {len(args): 1}
    in_specs += xs
    args += xa
    out_specs = [pl.BlockSpec((RT, D_MODEL), lambda b, t: (b * nt + t, 0)), st_spec]
    out_shape = [jax.ShapeDtypeStruct((n, D_MODEL), F32), st_out]
    return pl.pallas_call(
        functools.partial(_mixc_body, RT=RT, C=C, lseg=lseg, layer=layer, has_init=has_init, n_prev=len(xa)),
        grid=(n // (RT * nt), nt),
        in_specs=in_specs, out_specs=out_specs, out_shape=out_shape,
        input_output_aliases=aliases,
        scratch_shapes=[pltpu.VMEM((NS,) + st_shape, F32), pltpu.VMEM((C, D_MODEL), F32),
                        pltpu.VMEM((C, D_MODEL), F32), pltpu.VMEM((C, D_MODEL), F32),
                        pltpu.VMEM((C, D_MODEL), F32), pltpu.VMEM((HG_HEADS, C, HG_DK), F32)],
        compiler_params=_cparams(2),
        name="mix_hgrn",
    )(*args)


def _post_body(*refs, final):
    if final:
        (h_ref, a_ref, b_ref, c_ref, wo_ref, nw_ref, wg_ref, wu_ref, wd_ref, nf_ref, hout_ref, y_ref) = refs
    else:
        (h_ref, a_ref, b_ref, c_ref, wo_ref, nw_ref, wg_ref, wu_ref, wd_ref, hout_ref) = refs
    mix = (a_ref[...] + b_ref[...]) + c_ref[...]
    h1 = h_ref[...] + _dot(mix.astype(BF16), wo_ref[...])
    u = _rms(h1, nw_ref[...]).astype(BF16)
    act = (jax.nn.silu(_dot(u, wg_ref[...])) * _dot(u, wu_ref[...])).astype(BF16)
    h2 = h1 + _dot(act, wd_ref[...])
    hout_ref[...] = h2
    if final:
        y_ref[...] = _rms(h2, nf_ref[...])


def _post(h, oa, ob, oc, P, layer, tm, final):
    n = h.shape[0]
    row = pl.BlockSpec((tm, D_MODEL), lambda i: (i, 0))

    def wspec(shape):
        return pl.BlockSpec((None,) + shape, lambda i: (layer, 0, 0), pipeline_mode=pl.Buffered(1))

    in_specs = [row, row, row, row,
                wspec((D_MODEL, D_MODEL)), wspec((1, D_MODEL)),
                wspec((D_MODEL, D_FF)), wspec((D_MODEL, D_FF)), wspec((D_FF, D_MODEL))]
    args = [h, oa, ob, oc, P["w_out"], P["norm_ffn_w"], P["w_gate"], P["w_up"], P["w_down"]]
    out_specs = [row]
    out_shape = [jax.ShapeDtypeStruct((n, D_MODEL), F32)]
    if final:
        in_specs.append(pl.BlockSpec((1, D_MODEL), lambda i: (0, 0)))
        args.append(P["norm_final_w"])
        out_specs.append(row)
        out_shape.append(jax.ShapeDtypeStruct((n, D_MODEL), F32))
    return pl.pallas_call(
        functools.partial(_post_body, final=final),
        grid=(n // tm,),
        in_specs=in_specs, out_specs=out_specs, out_shape=out_shape,
        compiler_params=_cparams(1),
        name="post",
    )(*args)


def _trunk_cfg(B, L):
    n = B * L
    if L >= 256:
        return dict(tm=min(n, 1024), NB=1, TA=256, TB=256, RC=256, CC=64, LS=64, tp=min(n, 512))
    nbc = min(B, 64 // L)
    return dict(tm=min(n, 512), NB=min(B, 8), TA=L, TB=L, RC=nbc * L, CC=nbc * L, LS=L, tp=min(n, 512))


def _run_trunk(x, pos0, init, P):
    B, L, _ = x.shape
    cfg = _trunk_cfg(B, L)
    n = B * L
    h = x.reshape(n, D_MODEL).astype(F32)
    rtab = _ret_tables(cfg["TB"], L, pos0)
    htab = _hg_tables(cfg["CC"], cfg["LS"])
    conv = lru = ret = hg = None
    y = None
    for l in range(DEPTH):
        z = _proj(h, P["norm_mix_w"], P["w_in"], l, cfg["tm"], D_MODEL)
        z3 = z.reshape(B, L, IN_COLS)
        ia = None if init is None else (init[0], init[1])
        pa = None if l == 0 else (conv, lru)
        oa, conv, lru = _mixa(z3, P, l, ia, pa, cfg["TA"], cfg["NB"], pos0)
        ob, ret = _mixb(z3, rtab, l, None if init is None else init[2], ret, cfg["TB"], cfg["NB"])
        oc, hg = _mixc(z, P, htab, l, None if init is None else init[3], hg, B, L, cfg["RC"], cfg["CC"], cfg["LS"])
        res = _post(h, oa.reshape(n, D_MODEL), ob.reshape(n, D_MODEL), oc, P, l, cfg["tp"], l == DEPTH - 1)
        h = res[0]
        if l == DEPTH - 1:
            y = res[1]
    return y.reshape(B, L, D_MODEL), conv, lru.reshape(DEPTH, B, D_MODEL), ret, hg


def kernel(x_prompt, x_sample, state_conv, state_lru, state_ret, state_hgrn, norm_mix_w, w_in, conv_w, conv_b, lru_wa, lru_ba, lru_wx, lru_bx, lru_lambda, hg_lb, hg_norm_w, w_out, norm_ffn_w, w_gate, w_up, w_down, norm_final_w):
    row = lambda a: a.astype(F32).reshape(DEPTH, 1, D_MODEL)
    P = dict(
        norm_mix_w=row(norm_mix_w), w_in=w_in.astype(BF16),
        conv_w=conv_w.astype(F32), conv_b=row(conv_b),
        lru_wa=lru_wa.astype(BF16), lru_ba=row(lru_ba), lru_wx=lru_wx.astype(BF16), lru_bx=row(lru_bx),
        lru_lambda=row(lru_lambda), hg_lb=hg_lb.astype(F32), hg_norm_w=row(hg_norm_w),
        w_out=w_out.astype(BF16), norm_ffn_w=row(norm_ffn_w),
        w_gate=w_gate.astype(BF16), w_up=w_up.astype(BF16), w_down=w_down.astype(BF16),
        norm_final_w=norm_final_w.astype(F32).reshape(1, D_MODEL))
    init_s = (state_conv.astype(F32), state_lru.astype(F32).reshape(DEPTH, -1, 1, D_MODEL),
              state_ret.astype(F32), state_hgrn.astype(F32))
    y_p, conv_p, lru_p, ret_p, hg_p = _run_trunk(x_prompt, 0, None, P)
    y_s, conv_s, lru_s, ret_s, hg_s = _run_trunk(x_sample, PAST_LEN, init_s, P)
    return (y_p.astype(x_prompt.dtype), y_s.astype(x_sample.dtype),
            conv_p.astype(state_conv.dtype), lru_p.astype(state_lru.dtype),
            ret_p.astype(state_ret.dtype), hg_p.astype(state_hgrn.dtype),
            conv_s.astype(state_conv.dtype), lru_s.astype(state_lru.dtype),
            ret_s.astype(state_ret.dtype), hg_s.astype(state_hgrn.dtype))
```

```python
import functools

import numpy as np
import jax
import jax.numpy as jnp
from jax import lax
from jax.experimental import pallas as pl
from jax.experimental.pallas import tpu as pltpu

F32 = jnp.float32
BF16 = jnp.bfloat16

D_MODEL = 1024
DEPTH = 4
PAST_LEN = 16384
LRU_BLOCKS = 4
LRU_BW = D_MODEL // LRU_BLOCKS
CONV_W = 4
LRU_C = 8.0
RET_HEADS = 4
RET_DK = 256
RET_DV = 256
ROPE_BASE = 10000.0
HG_DK = 128
HG_HEADS = 8
HG_DV = 128
D_FF = 2816
EPS = 1e-6
GN_EPS = 1e-5
LOG2E = 1.4426950408889634
N_GROUPS = 13
IN_COLS = N_GROUPS * D_MODEL
G_AX, G_AY, G_RQ, G_RK, G_RV, G_RG, G_CQ, G_CF, G_CI, G_CG, G_MA, G_MB, G_MC = range(13)
GROUPS_A = (G_AX, G_AY, G_MA)
GROUPS_B = (G_RQ, G_RK, G_RV, G_RG, G_MB)
GROUPS_C = (G_CQ, G_CF, G_CI, G_CG, G_MC)

VMEM_LIMIT = 56 * 1024 * 1024
SUBLANES = 8
LANES = 128


def _cparams(n_axes):
    return pltpu.CompilerParams(dimension_semantics=("arbitrary",) * n_axes,
                                vmem_limit_bytes=VMEM_LIMIT)


def _rms(x, w):
    return x * lax.rsqrt(jnp.mean(x * x, axis=-1, keepdims=True) + EPS) * w


def _dot(a, b):
    return jnp.dot(a, b, preferred_element_type=F32)


def _dot_nt(a, b):
    return lax.dot_general(a, b, (((1,), (1,)), ((), ())), preferred_element_type=F32)


def _dot_tn(a, b):
    return lax.dot_general(a, b, (((0,), (0,)), ((), ())), preferred_element_type=F32)


def _stacked_state(prev, shape_tail, nb, layer, batch, bidx):
    nd = len(shape_tail)
    out_spec = pl.BlockSpec((None, nb) + shape_tail, lambda *g: (layer, bidx(*g)) + (0,) * nd)
    out_shape = jax.ShapeDtypeStruct((DEPTH, batch) + shape_tail, F32)
    if prev is None:
        return [], [], out_spec, out_shape
    return [pl.BlockSpec(memory_space=pl.ANY)], [prev], out_spec, out_shape


def _grid_maps(fused, n_batch_blocks, nt):
    if not fused:
        return (n_batch_blocks, nt), (lambda b, t: b), (lambda b, t: t), None, None
    n_steps = n_batch_blocks * nt
    nxt = lambda s: jnp.minimum(s + 1, n_steps - 1)
    return ((n_steps,), (lambda s: lax.div(s, nt)), (lambda s: lax.rem(s, nt)),
            (lambda s: lax.div(nxt(s), nt)), (lambda s: lax.rem(nxt(s), nt)))


def _wspecs(layer, groups):
    return [pl.BlockSpec((None, D_MODEL, D_MODEL), lambda *_, g=g: (layer, 0, g), pipeline_mode=pl.Buffered(1))
            for g in groups]


def _project(h, nw_ref, w_refs, dst):
    u = _rms(h, nw_ref[...]).astype(BF16)
    for j, w_ref in enumerate(w_refs):
        dst[:, j * D_MODEL:(j + 1) * D_MODEL] = _dot(u, w_ref[...])


def _fused_run(h_ref, hn_ref, nw_ref, w_refs, z_bufs, squeeze, main):
    s = pl.program_id(0)
    rd = (lambda r: r[0]) if squeeze else (lambda r: r[...])

    @pl.when(s == 0)
    def _():
        _project(rd(h_ref), nw_ref, w_refs, z_bufs[0])

    for parity in range(2):
        @pl.when(lax.rem(s, 2) == parity)
        def _():
            _project(rd(hn_ref), nw_ref, w_refs, z_bufs[1 - parity])
            main(z_bufs[parity])


def _proj_body(h_ref, nw_ref, w_ref, z_ref, u_scr):
    @pl.when(pl.program_id(1) == 0)
    def _():
        u_scr[...] = _rms(h_ref[...], nw_ref[...]).astype(BF16)

    z_ref[...] = _dot(u_scr[...], w_ref[...])


def _proj(h, nw_all, w_all, layer, tm, tn):
    n = h.shape[0]
    ncols = w_all.shape[2]
    return pl.pallas_call(
        _proj_body,
        grid=(n // tm, ncols // tn),
        in_specs=[
            pl.BlockSpec((tm, D_MODEL), lambda i, j: (i, 0)),
            pl.BlockSpec((None, 1, D_MODEL), lambda i, j: (layer, 0, 0)),
            pl.BlockSpec((None, D_MODEL, tn), lambda i, j: (layer, 0, j)),
        ],
        out_specs=pl.BlockSpec((tm, tn), lambda i, j: (i, j)),
        out_shape=jax.ShapeDtypeStruct((n, ncols), F32),
        scratch_shapes=[pltpu.VMEM((tm, D_MODEL), BF16)],
        compiler_params=_cparams(2),
        name="proj",
    )(h, nw_all, w_all)


def _mixa_body(*refs, T, NB, nt, pos0, has_init, n_prev, fused):
    n_in = 6 if fused else 3
    (cw_ref, cb_ref, wa_ref, ba_ref, wx_ref, bx_ref, lam_ref) = refs[n_in:n_in + 7]
    k = n_in + 7
    if has_init:
        conv0_ref, lru0_ref = refs[k:k + 2]
        k += 2
    k += n_prev
    out_ref, conv_out_ref, lru_out_ref, xpad, hcar, a_scr, u_scr, h_scr, g_scr = refs[k:k + 9]
    t = lax.rem(pl.program_id(0), nt) if fused else pl.program_id(1)

    @pl.when(t == 0)
    def _():
        if has_init:
            xpad[:, 5:8, :] = conv0_ref[...]
            hcar[...] = lru0_ref[...]
        else:
            xpad[:, 5:8, :] = jnp.zeros((NB, 3, D_MODEL), F32)
            hcar[...] = jnp.zeros((NB, 1, D_MODEL), F32)

    sp = jax.nn.softplus(-lam_ref[...])
    cw = cw_ref[...]
    cb = cb_ref[...]
    rowpos = pos0 + t * T + lax.broadcasted_iota(jnp.int32, (T, 1), 0)

    def one(ib, col):
        xp = xpad.at[ib]
        x = col(ib, 0)
        if T >= SUBLANES:
            xcat = jnp.concatenate([xp[0:SUBLANES, :], x], axis=0)
            xc = cb + cw[0:1] * pltpu.roll(xcat, 3, 0)[SUBLANES:]
            xc = xc + cw[1:2] * pltpu.roll(xcat, 2, 0)[SUBLANES:]
            xc = xc + cw[2:3] * pltpu.roll(xcat, 1, 0)[SUBLANES:]
            xc = xc + cw[3:4] * x
            xp[0:SUBLANES, :] = x[T - SUBLANES:, :]
            conv_out_ref[ib] = xp[5:8, :]
        else:
            xp[8:8 + T, :] = x
            xc = cb + cw[0:1] * xp[5:5 + T, :]
            xc = xc + cw[1:2] * xp[6:6 + T, :]
            xc = xc + cw[2:3] * xp[7:7 + T, :]
            xc = xc + cw[3:4] * x
            tail = xp[T + 5:T + 8, :]
            xp[5:8, :] = tail
            conv_out_ref[ib] = tail

        xcb = xc.astype(BF16)
        ra = jnp.concatenate(
            [_dot(xcb[:, n * LRU_BW:(n + 1) * LRU_BW], wa_ref[n]) for n in range(LRU_BLOCKS)], axis=1)
        rx = jnp.concatenate(
            [_dot(xcb[:, n * LRU_BW:(n + 1) * LRU_BW], wx_ref[n]) for n in range(LRU_BLOCKS)], axis=1)
        r = jax.nn.sigmoid(ra + ba_ref[...])
        i = jax.nn.sigmoid(rx + bx_ref[...])
        log_a = -LRU_C * r * sp
        a = jnp.exp(log_a)
        th = jnp.tanh(log_a)
        mult = jnp.sqrt(-2.0 * th / (1.0 - th))
        mult = jnp.where(rowpos == 0, 1.0, mult)
        a_scr[...] = a
        u_scr[...] = xc * i * mult
        g_scr[...] = jax.nn.sigmoid(col(ib, 2)) * jax.nn.gelu(col(ib, 1))

        def step(s, h):
            h = a_scr[pl.ds(s, 1), :] * h + u_scr[pl.ds(s, 1), :]
            h_scr[pl.ds(s, 1), :] = h
            return h

        h_last = lax.fori_loop(0, T, step, hcar[ib], unroll=min(T, 8))
        hcar[ib] = h_last
        lru_out_ref[ib] = h_last
        out_ref[ib] = g_scr[...] * h_scr[...]

    def main(zc):
        if fused:
            col = lambda ib, j: zc[:, j * D_MODEL:(j + 1) * D_MODEL]
        else:
            col = lambda ib, j: refs[j][ib]
        if NB == 1:
            one(0, col)
        else:
            def body(ib, c):
                one(ib, col)
                return c
            lax.fori_loop(0, NB, body, 0)

    if fused:
        _fused_run(refs[0], refs[1], refs[2], refs[3:6], refs[k + 9:k + 11], True, main)
    else:
        main(None)


def _mixa(src, P, layer, init, prev, B, L, T, NB, pos0, fused):
    has_init = init is not None
    nt = L // T
    grid, bidx, tidx, bnxt, tnxt = _grid_maps(fused, B // NB, nt)
    blk = lambda bf, tf, g=0: pl.BlockSpec((NB, T, D_MODEL), lambda *a: (bf(*a), tf(*a), g))
    pspec = lambda shape: pl.BlockSpec((None,) + shape, lambda *a: (layer,) + (0,) * len(shape))

    if fused:
        in_specs = [blk(bidx, tidx), blk(bnxt, tnxt), pspec((1, D_MODEL))] + _wspecs(layer, GROUPS_A)
        args = [src, src, P["norm_mix_w"]] + [P["w_in"]] * 3
    else:
        in_specs = [blk(bidx, tidx, g) for g in GROUPS_A]
        args = [src] * 3
    in_specs += [pspec((CONV_W, D_MODEL)), pspec((1, D_MODEL)),
                 pspec((LRU_BLOCKS, LRU_BW, LRU_BW)), pspec((1, D_MODEL)),
                 pspec((LRU_BLOCKS, LRU_BW, LRU_BW)), pspec((1, D_MODEL)), pspec((1, D_MODEL))]
    args += [P["conv_w"], P["conv_b"], P["lru_wa"], P["lru_ba"], P["lru_wx"], P["lru_bx"], P["lru_lambda"]]
    if has_init:
        conv0, lru0 = init
        in_specs += [pl.BlockSpec((None, NB, CONV_W - 1, D_MODEL), lambda *a: (layer, bidx(*a), 0, 0)),
                     pl.BlockSpec((None, NB, 1, D_MODEL), lambda *a: (layer, bidx(*a), 0, 0))]
        args += [conv0, lru0]
    pc, pl_ = (None, None) if prev is None else prev
    xs1, xa1, conv_spec, conv_shape = _stacked_state(pc, (CONV_W - 1, D_MODEL), NB, layer, B, bidx)
    xs2, xa2, lru_spec, lru_shape = _stacked_state(pl_, (1, D_MODEL), NB, layer, B, bidx)
    aliases = {} if prev is None else {len(args): 1, len(args) + 1: 2}
    in_specs += xs1 + xs2
    args += xa1 + xa2
    out_specs = [blk(bidx, tidx), conv_spec, lru_spec]
    out_shape = [jax.ShapeDtypeStruct((B, L, D_MODEL), F32), conv_shape, lru_shape]
    scratch = [pltpu.VMEM((NB, T + 8, D_MODEL), F32), pltpu.VMEM((NB, 1, D_MODEL), F32),
               pltpu.VMEM((T, D_MODEL), F32), pltpu.VMEM((T, D_MODEL), F32), pltpu.VMEM((T, D_MODEL), F32),
               pltpu.VMEM((T, D_MODEL), F32)]
    if fused:
        scratch += [pltpu.VMEM((T, len(GROUPS_A) * D_MODEL), F32)] * 2
    return pl.pallas_call(
        functools.partial(_mixa_body, T=T, NB=NB, nt=nt, pos0=pos0, has_init=has_init,
                          n_prev=len(xa1 + xa2), fused=fused),
        grid=grid, in_specs=in_specs, out_specs=out_specs, out_shape=out_shape,
        input_output_aliases=aliases, scratch_shapes=scratch,
        compiler_params=_cparams(len(grid)),
        name="mix_lru",
    )(*args)


def _mixb_body(*refs, NB, nt, has_init, n_prev, fused):
    n_in = 8 if fused else 5
    (cos_ref, sin_ref, d_ref, xi_ref, zeta_ref, gc_ref) = refs[n_in:n_in + 6]
    k = n_in + 6
    if has_init:
        r0_ref = refs[k]
        k += 1
    k += n_prev
    out_ref, r_out_ref, r_scr = refs[k:k + 3]
    t = lax.rem(pl.program_id(0), nt) if fused else pl.program_id(1)

    @pl.when(t == 0)
    def _():
        if has_init:
            r_scr[...] = r0_ref[...]
        else:
            r_scr[...] = jnp.zeros(r_scr.shape, F32)

    cos = cos_ref[...]
    sin = sin_ref[...]
    half = RET_DK // 2

    def rot(x):
        x1, x2 = x[:, :half], x[:, half:]
        return jnp.concatenate([x1 * cos - x2 * sin, x1 * sin + x2 * cos], axis=1)

    def one(ib, col):
        rr = r_scr.at[ib]
        for h in range(RET_HEADS):
            sl = slice(h * RET_DK, (h + 1) * RET_DK)
            qr = rot(col(ib, 0, sl))
            kr = rot(col(ib, 1, sl)) * RET_DK ** -0.5
            qb = qr.astype(BF16)
            vb = col(ib, 2, sl).astype(BF16)
            s = _dot_nt(qb, kr.astype(BF16)) * d_ref[h]
            rh = rr[h]
            o = _dot(s.astype(BF16), vb) + _dot(qb, rh.astype(BF16)) * xi_ref[:, sl]
            kz = (kr * zeta_ref[:, sl]).astype(BF16)
            rr[h] = gc_ref[:, sl] * rh + _dot_tn(kz, vb)
            d = o - jnp.mean(o, axis=-1, keepdims=True)
            gn = d * lax.rsqrt(jnp.mean(d * d, axis=-1, keepdims=True) + GN_EPS)
            out_ref[ib, :, sl] = jax.nn.sigmoid(col(ib, 4, sl)) * (gn * jax.nn.silu(col(ib, 3, sl)))
        r_out_ref[ib] = rr[...]

    def main(zc):
        if fused:
            col = lambda ib, j, sl: zc[:, j * D_MODEL + sl.start:j * D_MODEL + sl.stop]
        else:
            col = lambda ib, j, sl: refs[j][ib, :, sl]
        if NB == 1:
            one(0, col)
        else:
            def body(ib, c):
                one(ib, col)
                return c
            lax.fori_loop(0, NB, body, 0)

    if fused:
        _fused_run(refs[0], refs[1], refs[2], refs[3:8], refs[k + 3:k + 5], True, main)
    else:
        main(None)


def _ret_tables(C, L, pos0):
    t = jnp.arange(C, dtype=F32)
    log_g = jnp.log1p(-jnp.exp2(-5.0 - jnp.arange(RET_HEADS, dtype=F32)))
    rel = t[:, None] - t[None, :]
    dmat = jnp.exp(jnp.where(rel[None] >= 0, rel[None] * log_g[:, None, None], -jnp.inf))
    xi = jnp.exp((t[None, :] + 1.0) * log_g[:, None]).T
    zeta = jnp.exp((C - 1.0 - t[None, :]) * log_g[:, None]).T
    g_c = jnp.exp(C * log_g)
    rep = lambda a: jnp.repeat(a, RET_DV, axis=-1)
    half = RET_DK // 2
    pos = pos0 + jnp.arange(L, dtype=jnp.int32)
    inv = ROPE_BASE ** (-jnp.arange(half, dtype=F32) / half)
    ang = pos.astype(F32)[:, None] * inv[None, :]
    return dict(d=dmat, xi=rep(xi), zeta=rep(zeta), gc=rep(g_c[None, :]), cos=jnp.cos(ang), sin=jnp.sin(ang))


def _mixb(src, P, tab, layer, init, prev, B, L, T, NB, fused):
    has_init = init is not None
    nt = L // T
    grid, bidx, tidx, bnxt, tnxt = _grid_maps(fused, B // NB, nt)
    blk = lambda bf, tf, g=0: pl.BlockSpec((NB, T, D_MODEL), lambda *a: (bf(*a), tf(*a), g))
    const = lambda shape: pl.BlockSpec(shape, lambda *a: (0,) * len(shape))

    if fused:
        in_specs = ([blk(bidx, tidx), blk(bnxt, tnxt), pl.BlockSpec((None, 1, D_MODEL), lambda *a: (layer, 0, 0))]
                    + _wspecs(layer, GROUPS_B))
        args = [src, src, P["norm_mix_w"]] + [P["w_in"]] * 5
    else:
        in_specs = [blk(bidx, tidx, g) for g in GROUPS_B]
        args = [src] * 5
    in_specs += [pl.BlockSpec((T, RET_DK // 2), lambda *a: (tidx(*a), 0)),
                 pl.BlockSpec((T, RET_DK // 2), lambda *a: (tidx(*a), 0)),
                 const((RET_HEADS, T, T)), const((T, D_MODEL)), const((T, D_MODEL)), const((1, D_MODEL))]
    args += [tab["cos"], tab["sin"], tab["d"], tab["xi"], tab["zeta"], tab["gc"]]
    st_shape = (RET_HEADS, RET_DK, RET_DV)
    if has_init:
        in_specs.append(pl.BlockSpec((None, NB) + st_shape, lambda *a: (layer, bidx(*a), 0, 0, 0)))
        args.append(init)
    xs, xa, st_spec, st_out = _stacked_state(prev, st_shape, NB, layer, B, bidx)
    aliases = {} if prev is None else {len(args): 1}
    in_specs += xs
    args += xa
    out_specs = [blk(bidx, tidx), st_spec]
    out_shape = [jax.ShapeDtypeStruct((B, L, D_MODEL), F32), st_out]
    scratch = [pltpu.VMEM((NB,) + st_shape, F32)]
    if fused:
        scratch += [pltpu.VMEM((T, len(GROUPS_B) * D_MODEL), F32)] * 2
    return pl.pallas_call(
        functools.partial(_mixb_body, NB=NB, nt=nt, has_init=has_init, n_prev=len(xa), fused=fused),
        grid=grid, in_specs=in_specs, out_specs=out_specs, out_shape=out_shape,
        input_output_aliases=aliases, scratch_shapes=scratch,
        compiler_params=_cparams(len(grid)),
        name="mix_ret",
    )(*args)


def _hg_levels(lseg):
    ms = []
    m = lseg // 2
    while m >= 1:
        ms.append(m)
        m //= 2
    return ms


def _hg_tables(C, lseg):
    t = np.arange(C)
    masks, sgn = [], []
    for m in _hg_levels(lseg):
        blk = 2 * m
        same = (t[:, None] // blk) == (t[None, :] // blk)
        masks.append(same & ((t[:, None] % blk) >= m) & ((t[None, :] % blk) < m))
        if m < SUBLANES:
            sgn.append(np.where((t % blk) >= m, 1.0, -1.0)[:, None] * np.ones((1, LANES)))
    masks.append(np.eye(C, dtype=bool))
    masks = np.stack(masks).astype(np.float32)
    masks2 = np.concatenate([masks, masks], axis=2)
    seg = t // lseg
    ltri = ((seg[:, None] == seg[None, :]) & (t[None, :] <= t[:, None])).astype(np.float32)
    sgn = np.stack(sgn).astype(np.float32) if sgn else np.ones((1, C, LANES), np.float32)
    return dict(masks=jnp.asarray(masks2), ltri=jnp.asarray(ltri), sgn=jnp.asarray(sgn))


def _row_bcast(ref, C, blk, off):
    W = ref.shape[1]
    if blk >= SUBLANES:
        pieces = [jnp.broadcast_to(ref[j * blk + off:j * blk + off + 1, :], (blk, W)) for j in range(C // blk)]
    else:
        rowi = lax.broadcasted_iota(jnp.int32, (SUBLANES, W), 0)
        pieces = []
        for g in range(C // SUBLANES):
            acc = None
            for j in range(SUBLANES // blk):
                r = g * SUBLANES + j * blk + off
                bc = jnp.broadcast_to(ref[r:r + 1, :], (SUBLANES, W))
                acc = bc if acc is None else jnp.where(rowi >= j * blk, bc, acc)
            pieces.append(acc)
    return pieces[0] if len(pieces) == 1 else jnp.concatenate(pieces, axis=0)


def _mixc_body(*refs, RT, C, lseg, nt, layer, has_init, n_prev, fused):
    n_in = 8 if fused else 5
    (lb_ref, nw_ref, ltri_ref, mask_ref, sgn_ref) = refs[n_in:n_in + 5]
    k = n_in + 5
    if has_init:
        s0_ref = refs[k]
        k += 1
    k += n_prev
    out_ref, s_out_ref, st_scr, b_scr, qe_scr, ke_scr, oi_scr, bh_scr = refs[k:k + 8]
    t = lax.rem(pl.program_id(0), nt) if fused else pl.program_id(1)
    NS = C // lseg
    NP = HG_HEADS // 2
    PW = 2 * HG_DK

    @pl.when(t == 0)
    def _():
        if has_init:
            st_scr[...] = s0_ref[...]
        else:
            st_scr[...] = jnp.zeros(st_scr.shape, F32)

    if layer > 0:
        x = lb_ref[...]
        e = jnp.exp(x - jnp.max(x, axis=0, keepdims=True))
        sm = e / jnp.sum(e, axis=0, keepdims=True)
        lb = sm[1:2]
        for j in range(2, layer + 1):
            lb = lb + sm[j:j + 1]
    levels = _hg_levels(lseg)

    def blockdiag_rows(pair):
        z = jnp.zeros((pair.shape[0], HG_DK), pair.dtype)
        return jnp.concatenate([jnp.concatenate([pair[:, :HG_DK], z], axis=1),
                                jnp.concatenate([z, pair[:, HG_DK:]], axis=1)], axis=0)

    def chunk(r0, col, colh):
        rows = pl.ds(r0, C)
        cf = col(1, rows)
        if layer == 0:
            logf = jax.nn.log_sigmoid(cf)
            kk = jax.nn.sigmoid(-cf)
        else:
            sg = jax.nn.sigmoid(cf)
            logf = jnp.log(lb + (1.0 - lb) * sg)
            kk = (1.0 - lb) * (1.0 - sg)
        qq = jax.nn.silu(col(0, rows))
        vb = col(2, rows).astype(BF16)
        b2 = jnp.dot(ltri_ref[...], logf, precision=lax.Precision.HIGHEST, preferred_element_type=F32) * LOG2E
        b_scr[...] = b2
        bend = _row_bcast(b_scr, C, lseg, lseg - 1)
        qe = qq * jnp.exp2(b2)
        ke = kk * jnp.exp2(bend - b2)
        qb = qq.astype(BF16)
        kb = kk.astype(BF16)

        def level_w(li, m):
            blk = 2 * m
            if m >= SUBLANES:
                pieces = []
                for j in range(C // blk):
                    lo = slice(j * blk, j * blk + m)
                    up = slice(j * blk + m, (j + 1) * blk)
                    brow = b_scr[j * blk + m - 1:j * blk + m, :]
                    pieces.append(kk[lo] * jnp.exp2(brow - b2[lo]))
                    pieces.append(qq[up] * jnp.exp2(b2[up] - brow))
                w = jnp.concatenate(pieces, axis=0)
            else:
                si = li - (len(levels) - sgn_ref.shape[0])
                sgn = jnp.concatenate([sgn_ref[si]] * (D_MODEL // LANES), axis=1)
                e = jnp.exp2((b2 - _row_bcast(b_scr, C, blk, m - 1)) * sgn)
                w = jnp.where(sgn > 0.0, qq, kk) * e
            return w.astype(BF16)

        amat = [mask_ref[len(levels)] * _dot_nt(qb[:, p * PW:(p + 1) * PW], blockdiag_rows(kb[:, p * PW:(p + 1) * PW]))
                for p in range(NP)]
        for li, m in enumerate(levels):
            w = level_w(li, m)
            for p in range(NP):
                wp = w[:, p * PW:(p + 1) * PW]
                amat[p] = amat[p] + mask_ref[li] * _dot_nt(wp, blockdiag_rows(wp))
        o_intra = jnp.concatenate(
            [_dot(amat[p].astype(BF16), blockdiag_rows(vb[:, p * PW:(p + 1) * PW])) for p in range(NP)], axis=1)

        if NS == 1:
            bcol = b_scr[C - SUBLANES:C, :]
            o_parts = []
            for p in range(NP):
                sbd = blockdiag_rows(jnp.concatenate([st_scr[0, 2 * p], st_scr[0, 2 * p + 1]], axis=1))
                o_parts.append(_dot(qe[:, p * PW:(p + 1) * PW].astype(BF16), sbd.astype(BF16)))
            o_inter = jnp.concatenate(o_parts, axis=1)
            keb = ke.astype(BF16)
            for h in range(HG_HEADS):
                sl = slice(h * HG_DK, (h + 1) * HG_DK)
                dec = jnp.exp2(bcol[:, sl].T[:, SUBLANES - 1:SUBLANES])
                st_scr[0, h] = dec * st_scr[0, h] + _dot_tn(keb[:, sl], vb[:, sl])
        else:
            qe_scr[...] = qe
            ke_scr[...] = ke
            for h in range(HG_HEADS):
                sl = slice(h * HG_DK, (h + 1) * HG_DK)
                bh_scr[h] = b2[:, sl]
                bends = bh_scr[h, pl.ds(lseg - 1, NS, stride=lseg), :]
                dec_all = jnp.exp2(bends.T)
                for s in range(NS):
                    rs = slice(s * lseg, (s + 1) * lseg)
                    st = st_scr[s, h]
                    oi_scr[rs, sl] = _dot(qe_scr[rs, sl].astype(BF16), st.astype(BF16))
                    st_scr[s, h] = dec_all[:, s:s + 1] * st + _dot_tn(
                        ke_scr[rs, sl].astype(BF16), colh(2, pl.ds(r0 + s * lseg, lseg), sl).astype(BF16))
            o_inter = oi_scr[...]

        o = o_inter + o_intra
        gate = jax.nn.sigmoid(col(4, rows))
        og = nw_ref[...] * jax.nn.silu(col(3, rows))
        for h in range(HG_HEADS):
            sl = slice(h * HG_DK, (h + 1) * HG_DK)
            oh = o[:, sl]
            rms = oh * lax.rsqrt(jnp.mean(oh * oh, axis=-1, keepdims=True) + EPS)
            out_ref[rows, sl] = gate[:, sl] * (rms * og[:, sl])

    def main(zc):
        if fused:
            col = lambda j, rows: zc[rows, j * D_MODEL:(j + 1) * D_MODEL]
            colh = lambda j, rows, sl: zc[rows, j * D_MODEL + sl.start:j * D_MODEL + sl.stop]
            for c in range(RT // C):
                chunk(c * C, col, colh)
            return
        col = lambda j, rows: refs[j][rows, :]
        colh = lambda j, rows, sl: refs[j][rows, sl]
        if RT == C:
            chunk(0, col, colh)
        else:
            def cbody(c, carry):
                chunk(pl.multiple_of(c * C, C), col, colh)
                return carry
            lax.fori_loop(0, RT // C, cbody, 0)

    if fused:
        _fused_run(refs[0], refs[1], refs[2], refs[3:8], refs[k + 8:k + 10], False, main)
    else:
        main(None)

    s_out_ref[...] = st_scr[...]


def _mixc(src, P, tab, layer, init, prev, B, L, RT, C, lseg, fused):
    n = B * L
    has_init = init is not None
    NS = C // lseg
    nt = L // RT if lseg == C else 1
    nlev = len(_hg_levels(lseg)) + 1
    grid, bidx, tidx, bnxt, tnxt = _grid_maps(fused, n // (RT * nt), nt)
    blk = lambda bf, tf, g=0: pl.BlockSpec((RT, D_MODEL), lambda *a: (bf(*a) * nt + tf(*a), g))
    const = lambda shape: pl.BlockSpec(shape, lambda *a: (0,) * len(shape))

    if fused:
        in_specs = ([blk(bidx, tidx), blk(bnxt, tnxt), pl.BlockSpec((None, 1, D_MODEL), lambda *a: (layer, 0, 0))]
                    + _wspecs(layer, GROUPS_C))
        args = [src, src, P["norm_mix_w"]] + [P["w_in"]] * 5
    else:
        in_specs = [blk(bidx, tidx, g) for g in GROUPS_C]
        args = [src] * 5
    in_specs += [const((DEPTH, D_MODEL)), pl.BlockSpec((None, 1, D_MODEL), lambda *a: (layer, 0, 0)),
                 const((C, C)), const((nlev, C, 2 * C)), const(tab["sgn"].shape)]
    args += [P["hg_lb"], P["hg_norm_w"], tab["ltri"], tab["masks"], tab["sgn"]]
    st_shape = (HG_HEADS, HG_DK, HG_DV)
    if has_init:
        in_specs.append(pl.BlockSpec((None, NS) + st_shape, lambda *a: (layer, bidx(*a), 0, 0, 0)))
        args.append(init)
    xs, xa, st_spec, st_out = _stacked_state(prev, st_shape, NS, layer, B, bidx)
    aliases = {} if prev is None else {len(args): 1}
    in_specs += xs
    args += xa
    out_specs = [blk(bidx, tidx), st_spec]
    out_shape = [jax.ShapeDtypeStruct((n, D_MODEL), F32), st_out]
    scratch = [pltpu.VMEM((NS,) + st_shape, F32), pltpu.VMEM((C, D_MODEL), F32),
               pltpu.VMEM((C, D_MODEL), F32), pltpu.VMEM((C, D_MODEL), F32),
               pltpu.VMEM((C, D_MODEL), F32), pltpu.VMEM((HG_HEADS, C, HG_DK), F32)]
    if fused:
        scratch += [pltpu.VMEM((RT, len(GROUPS_C) * D_MODEL), F32)] * 2
    return pl.pallas_call(
        functools.partial(_mixc_body, RT=RT, C=C, lseg=lseg, nt=nt, layer=layer, has_init=has_init,
                          n_prev=len(xa), fused=fused),
        grid=grid, in_specs=in_specs, out_specs=out_specs, out_shape=out_shape,
        input_output_aliases=aliases, scratch_shapes=scratch,
        compiler_params=_cparams(len(grid)),
        name="mix_hgrn",
    )(*args)


def _post_body(*refs, final):
    if final:
        (h_ref, a_ref, b_ref, c_ref, wo_ref, nw_ref, wg_ref, wu_ref, wd_ref, nf_ref, hout_ref, y_ref) = refs
    else:
        (h_ref, a_ref, b_ref, c_ref, wo_ref, nw_ref, wg_ref, wu_ref, wd_ref, hout_ref) = refs
    mix = (a_ref[...] + b_ref[...]) + c_ref[...]
    h1 = h_ref[...] + _dot(mix.astype(BF16), wo_ref[...])
    u = _rms(h1, nw_ref[...]).astype(BF16)
    act = (jax.nn.silu(_dot(u, wg_ref[...])) * _dot(u, wu_ref[...])).astype(BF16)
    h2 = h1 + _dot(act, wd_ref[...])
    hout_ref[...] = h2
    if final:
        y_ref[...] = _rms(h2, nf_ref[...])


def _post(h, oa, ob, oc, P, layer, tm, final):
    n = h.shape[0]
    row = pl.BlockSpec((tm, D_MODEL), lambda i: (i, 0))

    def wspec(shape):
        return pl.BlockSpec((None,) + shape, lambda i: (layer, 0, 0), pipeline_mode=pl.Buffered(1))

    in_specs = [row, row, row, row,
                wspec((D_MODEL, D_MODEL)), wspec((1, D_MODEL)),
                wspec((D_MODEL, D_FF)), wspec((D_MODEL, D_FF)), wspec((D_FF, D_MODEL))]
    args = [h, oa, ob, oc, P["w_out"], P["norm_ffn_w"], P["w_gate"], P["w_up"], P["w_down"]]
    out_specs = [row]
    out_shape = [jax.ShapeDtypeStruct((n, D_MODEL), F32)]
    if final:
        in_specs.append(pl.BlockSpec((1, D_MODEL), lambda i: (0, 0)))
        args.append(P["norm_final_w"])
        out_specs.append(row)
        out_shape.append(jax.ShapeDtypeStruct((n, D_MODEL), F32))
    return pl.pallas_call(
        functools.partial(_post_body, final=final),
        grid=(n // tm,),
        in_specs=in_specs, out_specs=out_specs, out_shape=out_shape,
        compiler_params=_cparams(1),
        name="post",
    )(*args)


def _trunk_cfg(B, L):
    n = B * L
    if L >= 512:
        return dict(fused=True, tm=min(n, 1024), NB=1, TA=256, TB=256, RC=256, CC=64, LS=64, tp=min(n, 512))
    nbc = min(B, 64 // L)
    return dict(fused=False, tm=min(n, 512), NB=min(B, 8), TA=L, TB=L, RC=nbc * L, CC=nbc * L, LS=L,
                tp=min(n, 512))


def _run_trunk(x, pos0, init, P):
    B, L, _ = x.shape
    cfg = _trunk_cfg(B, L)
    fused = cfg["fused"]
    n = B * L
    h = x.reshape(n, D_MODEL).astype(F32)
    rtab = _ret_tables(cfg["TB"], L, pos0)
    htab = _hg_tables(cfg["CC"], cfg["LS"])
    conv = lru = ret = hg = None
    y = None
    for l in range(DEPTH):
        if fused:
            src2, src3 = h, h.reshape(B, L, D_MODEL)
        else:
            src2 = _proj(h, P["norm_mix_w"], P["w_in"], l, cfg["tm"], D_MODEL)
            src3 = src2.reshape(B, L, IN_COLS)
        ia = None if init is None else (init[0], init[1])
        pa = None if l == 0 else (conv, lru)
        oa, conv, lru = _mixa(src3, P, l, ia, pa, B, L, cfg["TA"], cfg["NB"], pos0, fused)
        ob, ret = _mixb(src3, P, rtab, l, None if init is None else init[2], ret, B, L, cfg["TB"], cfg["NB"], fused)
        oc, hg = _mixc(src2, P, htab, l, None if init is None else init[3], hg, B, L, cfg["RC"], cfg["CC"],
                       cfg["LS"], fused)
        res = _post(h, oa.reshape(n, D_MODEL), ob.reshape(n, D_MODEL), oc, P, l, cfg["tp"], l == DEPTH - 1)
        h = res[0]
        if l == DEPTH - 1:
            y = res[1]
    return y.reshape(B, L, D_MODEL), conv, lru.reshape(DEPTH, B, D_MODEL), ret, hg


def kernel(x_prompt, x_sample, state_conv, state_lru, state_ret, state_hgrn, norm_mix_w, w_in, conv_w, conv_b, lru_wa, lru_ba, lru_wx, lru_bx, lru_lambda, hg_lb, hg_norm_w, w_out, norm_ffn_w, w_gate, w_up, w_down, norm_final_w):
    row = lambda a: a.astype(F32).reshape(DEPTH, 1, D_MODEL)
    P = dict(
        norm_mix_w=row(norm_mix_w), w_in=w_in.astype(BF16),
        conv_w=conv_w.astype(F32), conv_b=row(conv_b),
        lru_wa=lru_wa.astype(BF16), lru_ba=row(lru_ba), lru_wx=lru_wx.astype(BF16), lru_bx=row(lru_bx),
        lru_lambda=row(lru_lambda), hg_lb=hg_lb.astype(F32), hg_norm_w=row(hg_norm_w),
        w_out=w_out.astype(BF16), norm_ffn_w=row(norm_ffn_w),
        w_gate=w_gate.astype(BF16), w_up=w_up.astype(BF16), w_down=w_down.astype(BF16),
        norm_final_w=norm_final_w.astype(F32).reshape(1, D_MODEL))
    init_s = (state_conv.astype(F32), state_lru.astype(F32).reshape(DEPTH, -1, 1, D_MODEL),
              state_ret.astype(F32), state_hgrn.astype(F32))
    y_p, conv_p, lru_p, ret_p, hg_p = _run_trunk(x_prompt, 0, None, P)
    y_s, conv_s, lru_s, ret_s, hg_s = _run_trunk(x_sample, PAST_LEN, init_s, P)
    return (y_p.astype(x_prompt.dtype), y_s.astype(x_sample.dtype),
            conv_p.astype(state_conv.dtype), lru_p.astype(state_lru.dtype),
            ret_p.astype(state_ret.dtype), hg_p.astype(state_hgrn.dtype),
            conv_s.astype(state_conv.dtype), lru_s.astype(state_lru.dtype),
            ret_s.astype(state_ret.dtype), hg_s.astype(state_hgrn.dtype))
```

```python
import functools

import numpy as np
import jax
import jax.numpy as jnp
from jax import lax
from jax.experimental import pallas as pl
from jax.experimental.pallas import tpu as pltpu

F32 = jnp.float32
BF16 = jnp.bfloat16

D_MODEL = 1024
DEPTH = 4
PAST_LEN = 16384
LRU_BLOCKS = 4
LRU_BW = D_MODEL // LRU_BLOCKS
CONV_W = 4
LRU_C = 8.0
RET_HEADS = 4
RET_DK = 256
RET_DV = 256
ROPE_BASE = 10000.0
HG_DK = 128
HG_HEADS = 8
HG_DV = 128
D_FF = 2816
EPS = 1e-6
GN_EPS = 1e-5
LOG2E = 1.4426950408889634
N_GROUPS = 13
IN_COLS = N_GROUPS * D_MODEL
G_AX, G_AY, G_RQ, G_RK, G_RV, G_RG, G_CQ, G_CF, G_CI, G_CG, G_MA, G_MB, G_MC = range(13)
GROUPS_A = (G_AX, G_AY, G_MA)
GROUPS_B = (G_RQ, G_RK, G_RV, G_RG, G_MB)
GROUPS_C = (G_CQ, G_CF, G_CI, G_CG, G_MC)

VMEM_LIMIT = 56 * 1024 * 1024
SUBLANES = 8
LANES = 128


def _cparams(n_axes):
    return pltpu.CompilerParams(dimension_semantics=("arbitrary",) * n_axes,
                                vmem_limit_bytes=VMEM_LIMIT)


def _rms(x, w):
    return x * lax.rsqrt(jnp.mean(x * x, axis=-1, keepdims=True) + EPS) * w


def _dot(a, b):
    return jnp.dot(a, b, preferred_element_type=F32)


def _dot_nt(a, b):
    return lax.dot_general(a, b, (((1,), (1,)), ((), ())), preferred_element_type=F32)


def _dot_tn(a, b):
    return lax.dot_general(a, b, (((0,), (0,)), ((), ())), preferred_element_type=F32)


def _stacked_state(prev, shape_tail, nb, layer, batch, bidx):
    nd = len(shape_tail)
    out_spec = pl.BlockSpec((None, nb) + shape_tail, lambda *g: (layer, bidx(*g)) + (0,) * nd)
    out_shape = jax.ShapeDtypeStruct((DEPTH, batch) + shape_tail, F32)
    if prev is None:
        return [], [], out_spec, out_shape
    return [pl.BlockSpec(memory_space=pl.ANY)], [prev], out_spec, out_shape


def _grid_maps(fused, n_batch_blocks, nt):
    if not fused:
        return (n_batch_blocks, nt), (lambda b, t: b), (lambda b, t: t), None, None
    n_steps = n_batch_blocks * nt
    nxt = lambda s: jnp.minimum(s + 1, n_steps - 1)
    return ((n_steps,), (lambda s: lax.div(s, nt)), (lambda s: lax.rem(s, nt)),
            (lambda s: lax.div(nxt(s), nt)), (lambda s: lax.rem(nxt(s), nt)))


def _wspecs(layer, groups):
    return [pl.BlockSpec((None, D_MODEL, D_MODEL), lambda *_, g=g: (layer, 0, g), pipeline_mode=pl.Buffered(1))
            for g in groups]


def _project(h, nw_ref, w_refs, dst):
    u = _rms(h, nw_ref[...]).astype(BF16)
    for j, w_ref in enumerate(w_refs):
        dst[:, j * D_MODEL:(j + 1) * D_MODEL] = _dot(u, w_ref[...])


def _project_pieces(h, nw_ref, w_refs, dst, width=256):
    u = _rms(h, nw_ref[...]).astype(BF16)

    def piece(j, c):
        def run():
            dst[:, j * D_MODEL + c:j * D_MODEL + c + width] = _dot(u, w_refs[j][:, c:c + width])
        return run

    return [piece(j, c) for j in range(len(w_refs)) for c in range(0, D_MODEL, width)]


def _fused_run(h_ref, hn_ref, nw_ref, w_refs, z_bufs, squeeze, main):
    s = pl.program_id(0)
    rd = (lambda r: r[0]) if squeeze else (lambda r: r[...])

    @pl.when(s == 0)
    def _():
        _project(rd(h_ref), nw_ref, w_refs, z_bufs[0])

    for parity in range(2):
        @pl.when(lax.rem(s, 2) == parity)
        def _():
            pieces = _project_pieces(rd(hn_ref), nw_ref, w_refs, z_bufs[1 - parity])

            def emit(k=None):
                for _ in range(len(pieces) if k is None else min(k, len(pieces))):
                    pieces.pop(0)()

            main(z_bufs[parity], emit)
            emit()


def _proj_body(h_ref, nw_ref, w_ref, z_ref, u_scr):
    @pl.when(pl.program_id(1) == 0)
    def _():
        u_scr[...] = _rms(h_ref[...], nw_ref[...]).astype(BF16)

    z_ref[...] = _dot(u_scr[...], w_ref[...])


def _proj(h, nw_all, w_all, layer, tm, tn):
    n = h.shape[0]
    ncols = w_all.shape[2]
    return pl.pallas_call(
        _proj_body,
        grid=(n // tm, ncols // tn),
        in_specs=[
            pl.BlockSpec((tm, D_MODEL), lambda i, j: (i, 0)),
            pl.BlockSpec((None, 1, D_MODEL), lambda i, j: (layer, 0, 0)),
            pl.BlockSpec((None, D_MODEL, tn), lambda i, j: (layer, 0, j)),
        ],
        out_specs=pl.BlockSpec((tm, tn), lambda i, j: (i, j)),
        out_shape=jax.ShapeDtypeStruct((n, ncols), F32),
        scratch_shapes=[pltpu.VMEM((tm, D_MODEL), BF16)],
        compiler_params=_cparams(2),
        name="proj",
    )(h, nw_all, w_all)


def _mixa_body(*refs, T, NB, nt, pos0, has_init, n_prev, fused):
    n_in = 6 if fused else 3
    (cw_ref, cb_ref, wa_ref, ba_ref, wx_ref, bx_ref, lam_ref) = refs[n_in:n_in + 7]
    k = n_in + 7
    if has_init:
        conv0_ref, lru0_ref = refs[k:k + 2]
        k += 2
    k += n_prev
    out_ref, conv_out_ref, lru_out_ref, xpad, hcar, a_scr, u_scr, h_scr, g_scr = refs[k:k + 9]
    t = lax.rem(pl.program_id(0), nt) if fused else pl.program_id(1)

    @pl.when(t == 0)
    def _():
        if has_init:
            xpad[:, 5:8, :] = conv0_ref[...]
            hcar[...] = lru0_ref[...]
        else:
            xpad[:, 5:8, :] = jnp.zeros((NB, 3, D_MODEL), F32)
            hcar[...] = jnp.zeros((NB, 1, D_MODEL), F32)

    sp = jax.nn.softplus(-lam_ref[...])
    cw = cw_ref[...]
    cb = cb_ref[...]
    rowpos = pos0 + t * T + lax.broadcasted_iota(jnp.int32, (T, 1), 0)

    RB = min(T, 64)

    def gates_and_inputs(xc, rows, zy, zm):
        xcb = xc.astype(BF16)
        ra = jnp.concatenate(
            [_dot(xcb[:, n * LRU_BW:(n + 1) * LRU_BW], wa_ref[n]) for n in range(LRU_BLOCKS)], axis=1)
        rx = jnp.concatenate(
            [_dot(xcb[:, n * LRU_BW:(n + 1) * LRU_BW], wx_ref[n]) for n in range(LRU_BLOCKS)], axis=1)
        r = jax.nn.sigmoid(ra + ba_ref[...])
        i = jax.nn.sigmoid(rx + bx_ref[...])
        log_a = -LRU_C * r * sp
        a = jnp.exp(log_a)
        th = jnp.tanh(log_a)
        mult = jnp.sqrt(-2.0 * th / (1.0 - th))
        mult = jnp.where(rowpos[rows] == 0, 1.0, mult)
        a_scr[rows, :] = a
        u_scr[rows, :] = xc * i * mult
        g_scr[rows, :] = jax.nn.sigmoid(zm) * jax.nn.gelu(zy)

    def one(ib, col, emit):
        xp = xpad.at[ib]
        if T >= SUBLANES:
            prev = xp[0:SUBLANES, :]
            for rb in range(T // RB):
                rows = slice(rb * RB, (rb + 1) * RB)
                x = col(ib, 0, rows)
                xcat = jnp.concatenate([prev, x], axis=0)
                xc = cb + cw[0:1] * pltpu.roll(xcat, 3, 0)[SUBLANES:]
                xc = xc + cw[1:2] * pltpu.roll(xcat, 2, 0)[SUBLANES:]
                xc = xc + cw[2:3] * pltpu.roll(xcat, 1, 0)[SUBLANES:]
                xc = xc + cw[3:4] * x
                prev = x[RB - SUBLANES:, :]
                gates_and_inputs(xc, rows, col(ib, 1, rows), col(ib, 2, rows))
                emit(3)
            xp[0:SUBLANES, :] = prev
            conv_out_ref[ib] = prev[5:8, :]
        else:
            rows = slice(0, T)
            x = col(ib, 0, rows)
            xp[8:8 + T, :] = x
            xc = cb + cw[0:1] * xp[5:5 + T, :]
            xc = xc + cw[1:2] * xp[6:6 + T, :]
            xc = xc + cw[2:3] * xp[7:7 + T, :]
            xc = xc + cw[3:4] * x
            tail = xp[T + 5:T + 8, :]
            xp[5:8, :] = tail
            conv_out_ref[ib] = tail
            gates_and_inputs(xc, rows, col(ib, 1, rows), col(ib, 2, rows))

        def step(s, h):
            h = a_scr[pl.ds(s, 1), :] * h + u_scr[pl.ds(s, 1), :]
            h_scr[pl.ds(s, 1), :] = h
            return h

        h_last = lax.fori_loop(0, T, step, hcar[ib], unroll=min(T, 8))
        hcar[ib] = h_last
        lru_out_ref[ib] = h_last
        out_ref[ib] = g_scr[...] * h_scr[...]

    def main(zc, emit):
        if fused:
            col = lambda ib, j, rows: zc[rows, j * D_MODEL:(j + 1) * D_MODEL]
        else:
            col = lambda ib, j, rows: refs[j][ib, rows, :]
            emit = lambda k: None
        if NB == 1:
            one(0, col, emit)
        else:
            def body(ib, c):
                one(ib, col, emit)
                return c
            lax.fori_loop(0, NB, body, 0)

    if fused:
        _fused_run(refs[0], refs[1], refs[2], refs[3:6], refs[k + 9:k + 11], True, main)
    else:
        main(None, None)


def _mixa(src, P, layer, init, prev, B, L, T, NB, pos0, fused):
    has_init = init is not None
    nt = L // T
    grid, bidx, tidx, bnxt, tnxt = _grid_maps(fused, B // NB, nt)
    blk = lambda bf, tf, g=0: pl.BlockSpec((NB, T, D_MODEL), lambda *a: (bf(*a), tf(*a), g))
    pspec = lambda shape: pl.BlockSpec((None,) + shape, lambda *a: (layer,) + (0,) * len(shape))

    if fused:
        in_specs = [blk(bidx, tidx), blk(bnxt, tnxt), pspec((1, D_MODEL))] + _wspecs(layer, GROUPS_A)
        args = [src, src, P["norm_mix_w"]] + [P["w_in"]] * 3
    else:
        in_specs = [blk(bidx, tidx, g) for g in GROUPS_A]
        args = [src] * 3
    in_specs += [pspec((CONV_W, D_MODEL)), pspec((1, D_MODEL)),
                 pspec((LRU_BLOCKS, LRU_BW, LRU_BW)), pspec((1, D_MODEL)),
                 pspec((LRU_BLOCKS, LRU_BW, LRU_BW)), pspec((1, D_MODEL)), pspec((1, D_MODEL))]
    args += [P["conv_w"], P["conv_b"], P["lru_wa"], P["lru_ba"], P["lru_wx"], P["lru_bx"], P["lru_lambda"]]
    if has_init:
        conv0, lru0 = init
        in_specs += [pl.BlockSpec((None, NB, CONV_W - 1, D_MODEL), lambda *a: (layer, bidx(*a), 0, 0)),
                     pl.BlockSpec((None, NB, 1, D_MODEL), lambda *a: (layer, bidx(*a), 0, 0))]
        args += [conv0, lru0]
    pc, pl_ = (None, None) if prev is None else prev
    xs1, xa1, conv_spec, conv_shape = _stacked_state(pc, (CONV_W - 1, D_MODEL), NB, layer, B, bidx)
    xs2, xa2, lru_spec, lru_shape = _stacked_state(pl_, (1, D_MODEL), NB, layer, B, bidx)
    aliases = {} if prev is None else {len(args): 1, len(args) + 1: 2}
    in_specs += xs1 + xs2
    args += xa1 + xa2
    out_specs = [blk(bidx, tidx), conv_spec, lru_spec]
    out_shape = [jax.ShapeDtypeStruct((B, L, D_MODEL), F32), conv_shape, lru_shape]
    scratch = [pltpu.VMEM((NB, T + 8, D_MODEL), F32), pltpu.VMEM((NB, 1, D_MODEL), F32),
               pltpu.VMEM((T, D_MODEL), F32), pltpu.VMEM((T, D_MODEL), F32), pltpu.VMEM((T, D_MODEL), F32),
               pltpu.VMEM((T, D_MODEL), F32)]
    if fused:
        scratch += [pltpu.VMEM((T, len(GROUPS_A) * D_MODEL), F32)] * 2
    return pl.pallas_call(
        functools.partial(_mixa_body, T=T, NB=NB, nt=nt, pos0=pos0, has_init=has_init,
                          n_prev=len(xa1 + xa2), fused=fused),
        grid=grid, in_specs=in_specs, out_specs=out_specs, out_shape=out_shape,
        input_output_aliases=aliases, scratch_shapes=scratch,
        compiler_params=_cparams(len(grid)),
        name="mix_lru",
    )(*args)


def _mixb_body(*refs, NB, nt, has_init, n_prev, fused):
    n_in = 8 if fused else 5
    (cos_ref, sin_ref, d_ref, xi_ref, zeta_ref, gc_ref) = refs[n_in:n_in + 6]
    k = n_in + 6
    if has_init:
        r0_ref = refs[k]
        k += 1
    k += n_prev
    out_ref, r_out_ref, r_scr = refs[k:k + 3]
    t = lax.rem(pl.program_id(0), nt) if fused else pl.program_id(1)

    @pl.when(t == 0)
    def _():
        if has_init:
            r_scr[...] = r0_ref[...]
        else:
            r_scr[...] = jnp.zeros(r_scr.shape, F32)

    cos = cos_ref[...]
    sin = sin_ref[...]
    half = RET_DK // 2

    def rot(x):
        x1, x2 = x[:, :half], x[:, half:]
        return jnp.concatenate([x1 * cos - x2 * sin, x1 * sin + x2 * cos], axis=1)

    def one(ib, col, emit):
        rr = r_scr.at[ib]
        for h in range(RET_HEADS):
            sl = slice(h * RET_DK, (h + 1) * RET_DK)
            qr = rot(col(ib, 0, sl))
            kr = rot(col(ib, 1, sl)) * RET_DK ** -0.5
            qb = qr.astype(BF16)
            vb = col(ib, 2, sl).astype(BF16)
            s = _dot_nt(qb, kr.astype(BF16)) * d_ref[h]
            emit(1)
            rh = rr[h]
            o = _dot(s.astype(BF16), vb) + _dot(qb, rh.astype(BF16)) * xi_ref[:, sl]
            emit(1)
            kz = (kr * zeta_ref[:, sl]).astype(BF16)
            rr[h] = gc_ref[:, sl] * rh + _dot_tn(kz, vb)
            emit(1)
            d = o - jnp.mean(o, axis=-1, keepdims=True)
            gn = d * lax.rsqrt(jnp.mean(d * d, axis=-1, keepdims=True) + GN_EPS)
            emit(1)
            out_ref[ib, :, sl] = jax.nn.sigmoid(col(ib, 4, sl)) * (gn * jax.nn.silu(col(ib, 3, sl)))
            emit(1)
        r_out_ref[ib] = rr[...]

    def main(zc, emit):
        if fused:
            col = lambda ib, j, sl: zc[:, j * D_MODEL + sl.start:j * D_MODEL + sl.stop]
        else:
            col = lambda ib, j, sl: refs[j][ib, :, sl]
            emit = lambda k: None
        if NB == 1:
            one(0, col, emit)
        else:
            def body(ib, c):
                one(ib, col, emit)
                return c
            lax.fori_loop(0, NB, body, 0)

    if fused:
        _fused_run(refs[0], refs[1], refs[2], refs[3:8], refs[k + 3:k + 5], True, main)
    else:
        main(None, None)


def _ret_tables(C, L, pos0):
    t = jnp.arange(C, dtype=F32)
    log_g = jnp.log1p(-jnp.exp2(-5.0 - jnp.arange(RET_HEADS, dtype=F32)))
    rel = t[:, None] - t[None, :]
    dmat = jnp.exp(jnp.where(rel[None] >= 0, rel[None] * log_g[:, None, None], -jnp.inf))
    xi = jnp.exp((t[None, :] + 1.0) * log_g[:, None]).T
    zeta = jnp.exp((C - 1.0 - t[None, :]) * log_g[:, None]).T
    g_c = jnp.exp(C * log_g)
    rep = lambda a: jnp.repeat(a, RET_DV, axis=-1)
    half = RET_DK // 2
    pos = pos0 + jnp.arange(L, dtype=jnp.int32)
    inv = ROPE_BASE ** (-jnp.arange(half, dtype=F32) / half)
    ang = pos.astype(F32)[:, None] * inv[None, :]
    return dict(d=dmat, xi=rep(xi), zeta=rep(zeta), gc=rep(g_c[None, :]), cos=jnp.cos(ang), sin=jnp.sin(ang))


def _mixb(src, P, tab, layer, init, prev, B, L, T, NB, fused):
    has_init = init is not None
    nt = L // T
    grid, bidx, tidx, bnxt, tnxt = _grid_maps(fused, B // NB, nt)
    blk = lambda bf, tf, g=0: pl.BlockSpec((NB, T, D_MODEL), lambda *a: (bf(*a), tf(*a), g))
    const = lambda shape: pl.BlockSpec(shape, lambda *a: (0,) * len(shape))

    if fused:
        in_specs = ([blk(bidx, tidx), blk(bnxt, tnxt), pl.BlockSpec((None, 1, D_MODEL), lambda *a: (layer, 0, 0))]
                    + _wspecs(layer, GROUPS_B))
        args = [src, src, P["norm_mix_w"]] + [P["w_in"]] * 5
    else:
        in_specs = [blk(bidx, tidx, g) for g in GROUPS_B]
        args = [src] * 5
    in_specs += [pl.BlockSpec((T, RET_DK // 2), lambda *a: (tidx(*a), 0)),
                 pl.BlockSpec((T, RET_DK // 2), lambda *a: (tidx(*a), 0)),
                 const((RET_HEADS, T, T)), const((T, D_MODEL)), const((T, D_MODEL)), const((1, D_MODEL))]
    args += [tab["cos"], tab["sin"], tab["d"], tab["xi"], tab["zeta"], tab["gc"]]
    st_shape = (RET_HEADS, RET_DK, RET_DV)
    if has_init:
        in_specs.append(pl.BlockSpec((None, NB) + st_shape, lambda *a: (layer, bidx(*a), 0, 0, 0)))
        args.append(init)
    xs, xa, st_spec, st_out = _stacked_state(prev, st_shape, NB, layer, B, bidx)
    aliases = {} if prev is None else {len(args): 1}
    in_specs += xs
    args += xa
    out_specs = [blk(bidx, tidx), st_spec]
    out_shape = [jax.ShapeDtypeStruct((B, L, D_MODEL), F32), st_out]
    scratch = [pltpu.VMEM((NB,) + st_shape, F32)]
    if fused:
        scratch += [pltpu.VMEM((T, len(GROUPS_B) * D_MODEL), F32)] * 2
    return pl.pallas_call(
        functools.partial(_mixb_body, NB=NB, nt=nt, has_init=has_init, n_prev=len(xa), fused=fused),
        grid=grid, in_specs=in_specs, out_specs=out_specs, out_shape=out_shape,
        input_output_aliases=aliases, scratch_shapes=scratch,
        compiler_params=_cparams(len(grid)),
        name="mix_ret",
    )(*args)


def _hg_levels(lseg):
    ms = []
    m = lseg // 2
    while m >= 1:
        ms.append(m)
        m //= 2
    return ms


def _hg_tables(C, lseg):
    t = np.arange(C)
    masks, sgn = [], []
    for m in _hg_levels(lseg):
        blk = 2 * m
        same = (t[:, None] // blk) == (t[None, :] // blk)
        masks.append(same & ((t[:, None] % blk) >= m) & ((t[None, :] % blk) < m))
        if m < SUBLANES:
            sgn.append(np.where((t % blk) >= m, 1.0, -1.0)[:, None] * np.ones((1, LANES)))
    masks.append(np.eye(C, dtype=bool))
    masks = np.stack(masks).astype(np.float32)
    masks2 = np.concatenate([masks, masks], axis=2)
    seg = t // lseg
    ltri = ((seg[:, None] == seg[None, :]) & (t[None, :] <= t[:, None])).astype(np.float32)
    sgn = np.stack(sgn).astype(np.float32) if sgn else np.ones((1, C, LANES), np.float32)
    return dict(masks=jnp.asarray(masks2), ltri=jnp.asarray(ltri).astype(BF16), sgn=jnp.asarray(sgn))


def _row_bcast(ref, C, blk, off):
    W = ref.shape[1]
    if blk >= SUBLANES:
        pieces = [jnp.broadcast_to(ref[j * blk + off:j * blk + off + 1, :], (blk, W)) for j in range(C // blk)]
    else:
        rowi = lax.broadcasted_iota(jnp.int32, (SUBLANES, W), 0)
        pieces = []
        for g in range(C // SUBLANES):
            acc = None
            for j in range(SUBLANES // blk):
                r = g * SUBLANES + j * blk + off
                bc = jnp.broadcast_to(ref[r:r + 1, :], (SUBLANES, W))
                acc = bc if acc is None else jnp.where(rowi >= j * blk, bc, acc)
            pieces.append(acc)
    return pieces[0] if len(pieces) == 1 else jnp.concatenate(pieces, axis=0)


def _mixc_body(*refs, RT, C, lseg, nt, layer, has_init, n_prev, fused):
    n_in = 8 if fused else 5
    (lb_ref, nw_ref, ltri_ref, mask_ref, sgn_ref) = refs[n_in:n_in + 5]
    k = n_in + 5
    if has_init:
        s0_ref = refs[k]
        k += 1
    k += n_prev
    out_ref, s_out_ref, st_scr, b_scr, qe_scr, ke_scr, oi_scr, bh_scr = refs[k:k + 8]
    t = lax.rem(pl.program_id(0), nt) if fused else pl.program_id(1)
    NS = C // lseg
    NP = HG_HEADS // 2
    PW = 2 * HG_DK

    @pl.when(t == 0)
    def _():
        if has_init:
            st_scr[...] = s0_ref[...]
        else:
            st_scr[...] = jnp.zeros(st_scr.shape, F32)

    if layer > 0:
        x = lb_ref[...]
        e = jnp.exp(x - jnp.max(x, axis=0, keepdims=True))
        sm = e / jnp.sum(e, axis=0, keepdims=True)
        lb = sm[1:2]
        for j in range(2, layer + 1):
            lb = lb + sm[j:j + 1]
    levels = _hg_levels(lseg)

    def blockdiag_rows(pair):
        z = jnp.zeros((pair.shape[0], HG_DK), pair.dtype)
        return jnp.concatenate([jnp.concatenate([pair[:, :HG_DK], z], axis=1),
                                jnp.concatenate([z, pair[:, HG_DK:]], axis=1)], axis=0)

    def chunk(r0, col, colh, emit):
        rows = pl.ds(r0, C)
        cf = col(1, rows)
        if layer == 0:
            logf = jax.nn.log_sigmoid(cf)
            kk = jax.nn.sigmoid(-cf)
        else:
            sg = jax.nn.sigmoid(cf)
            logf = jnp.log(lb + (1.0 - lb) * sg)
            kk = (1.0 - lb) * (1.0 - sg)
        qq = jax.nn.silu(col(0, rows))
        vb = col(2, rows).astype(BF16)
        hi = logf.astype(BF16)
        r1 = logf - hi.astype(F32)
        mid = r1.astype(BF16)
        lo = (r1 - mid.astype(F32)).astype(BF16)
        lt = ltri_ref[...]
        b2 = ((_dot(lt, lo) + _dot(lt, mid)) + _dot(lt, hi)) * LOG2E
        b_scr[...] = b2
        bend = _row_bcast(b_scr, C, lseg, lseg - 1)
        qe = qq * jnp.exp2(b2)
        ke = kk * jnp.exp2(bend - b2)
        qb = qq.astype(BF16)
        kb = kk.astype(BF16)

        def level_w(li, m):
            blk = 2 * m
            if m >= SUBLANES:
                pieces = []
                for j in range(C // blk):
                    lo = slice(j * blk, j * blk + m)
                    up = slice(j * blk + m, (j + 1) * blk)
                    brow = b_scr[j * blk + m - 1:j * blk + m, :]
                    pieces.append(kk[lo] * jnp.exp2(brow - b2[lo]))
                    pieces.append(qq[up] * jnp.exp2(b2[up] - brow))
                w = jnp.concatenate(pieces, axis=0)
            else:
                si = li - (len(levels) - sgn_ref.shape[0])
                sgn = jnp.concatenate([sgn_ref[si]] * (D_MODEL // LANES), axis=1)
                e = jnp.exp2((b2 - _row_bcast(b_scr, C, blk, m - 1)) * sgn)
                w = jnp.where(sgn > 0.0, qq, kk) * e
            return w.astype(BF16)

        amat = [mask_ref[len(levels)] * _dot_nt(qb[:, p * PW:(p + 1) * PW], blockdiag_rows(kb[:, p * PW:(p + 1) * PW]))
                for p in range(NP)]
        for li, m in enumerate(levels):
            w = level_w(li, m)
            for p in range(NP):
                wp = w[:, p * PW:(p + 1) * PW]
                amat[p] = amat[p] + mask_ref[li] * _dot_nt(wp, blockdiag_rows(wp))
            emit(1)
        o_intra = jnp.concatenate(
            [_dot(amat[p].astype(BF16), blockdiag_rows(vb[:, p * PW:(p + 1) * PW])) for p in range(NP)], axis=1)

        if NS == 1:
            bcol = b_scr[C - SUBLANES:C, :]
            o_parts = []
            for p in range(NP):
                sbd = blockdiag_rows(jnp.concatenate([st_scr[0, 2 * p], st_scr[0, 2 * p + 1]], axis=1))
                o_parts.append(_dot(qe[:, p * PW:(p + 1) * PW].astype(BF16), sbd.astype(BF16)))
            o_inter = jnp.concatenate(o_parts, axis=1)
            keb = ke.astype(BF16)
            for h in range(HG_HEADS):
                sl = slice(h * HG_DK, (h + 1) * HG_DK)
                dec = jnp.exp2(bcol[:, sl].T[:, SUBLANES - 1:SUBLANES])
                st_scr[0, h] = dec * st_scr[0, h] + _dot_tn(keb[:, sl], vb[:, sl])
        else:
            qe_scr[...] = qe
            ke_scr[...] = ke
            for h in range(HG_HEADS):
                sl = slice(h * HG_DK, (h + 1) * HG_DK)
                bh_scr[h] = b2[:, sl]
                bends = bh_scr[h, pl.ds(lseg - 1, NS, stride=lseg), :]
                dec_all = jnp.exp2(bends.T)
                for s in range(NS):
                    rs = slice(s * lseg, (s + 1) * lseg)
                    st = st_scr[s, h]
                    oi_scr[rs, sl] = _dot(qe_scr[rs, sl].astype(BF16), st.astype(BF16))
                    st_scr[s, h] = dec_all[:, s:s + 1] * st + _dot_tn(
                        ke_scr[rs, sl].astype(BF16), colh(2, pl.ds(r0 + s * lseg, lseg), sl).astype(BF16))
            o_inter = oi_scr[...]

        o = o_inter + o_intra
        gate = jax.nn.sigmoid(col(4, rows))
        og = nw_ref[...] * jax.nn.silu(col(3, rows))
        for h in range(HG_HEADS):
            sl = slice(h * HG_DK, (h + 1) * HG_DK)
            oh = o[:, sl]
            rms = oh * lax.rsqrt(jnp.mean(oh * oh, axis=-1, keepdims=True) + EPS)
            out_ref[rows, sl] = gate[:, sl] * (rms * og[:, sl])

    def main(zc, emit):
        if fused:
            col = lambda j, rows: zc[rows, j * D_MODEL:(j + 1) * D_MODEL]
            colh = lambda j, rows, sl: zc[rows, j * D_MODEL + sl.start:j * D_MODEL + sl.stop]
            for c in range(RT // C):
                chunk(c * C, col, colh, emit)
            return
        col = lambda j, rows: refs[j][rows, :]
        colh = lambda j, rows, sl: refs[j][rows, sl]
        emit = lambda k: None
        if RT == C:
            chunk(0, col, colh, emit)
        else:
            def cbody(c, carry):
                chunk(pl.multiple_of(c * C, C), col, colh, emit)
                return carry
            lax.fori_loop(0, RT // C, cbody, 0)

    if fused:
        _fused_run(refs[0], refs[1], refs[2], refs[3:8], refs[k + 8:k + 10], False, main)
    else:
        main(None, None)

    s_out_ref[...] = st_scr[...]


def _mixc(src, P, tab, layer, init, prev, B, L, RT, C, lseg, fused):
    n = B * L
    has_init = init is not None
    NS = C // lseg
    nt = L // RT if lseg == C else 1
    nlev = len(_hg_levels(lseg)) + 1
    grid, bidx, tidx, bnxt, tnxt = _grid_maps(fused, n // (RT * nt), nt)
    blk = lambda bf, tf, g=0: pl.BlockSpec((RT, D_MODEL), lambda *a: (bf(*a) * nt + tf(*a), g))
    const = lambda shape: pl.BlockSpec(shape, lambda *a: (0,) * len(shape))

    if fused:
        in_specs = ([blk(bidx, tidx), blk(bnxt, tnxt), pl.BlockSpec((None, 1, D_MODEL), lambda *a: (layer, 0, 0))]
                    + _wspecs(layer, GROUPS_C))
        args = [src, src, P["norm_mix_w"]] + [P["w_in"]] * 5
    else:
        in_specs = [blk(bidx, tidx, g) for g in GROUPS_C]
        args = [src] * 5
    in_specs += [const((DEPTH, D_MODEL)), pl.BlockSpec((None, 1, D_MODEL), lambda *a: (layer, 0, 0)),
                 const((C, C)), const((nlev, C, 2 * C)), const(tab["sgn"].shape)]
    args += [P["hg_lb"], P["hg_norm_w"], tab["ltri"], tab["masks"], tab["sgn"]]
    st_shape = (HG_HEADS, HG_DK, HG_DV)
    if has_init:
        in_specs.append(pl.BlockSpec((None, NS) + st_shape, lambda *a: (layer, bidx(*a), 0, 0, 0)))
        args.append(init)
    xs, xa, st_spec, st_out = _stacked_state(prev, st_shape, NS, layer, B, bidx)
    aliases = {} if prev is None else {len(args): 1}
    in_specs += xs
    args += xa
    out_specs = [blk(bidx, tidx), st_spec]
    out_shape = [jax.ShapeDtypeStruct((n, D_MODEL), F32), st_out]
    scratch = [pltpu.VMEM((NS,) + st_shape, F32), pltpu.VMEM((C, D_MODEL), F32),
               pltpu.VMEM((C, D_MODEL), F32), pltpu.VMEM((C, D_MODEL), F32),
               pltpu.VMEM((C, D_MODEL), F32), pltpu.VMEM((HG_HEADS, C, HG_DK), F32)]
    if fused:
        scratch += [pltpu.VMEM((RT, len(GROUPS_C) * D_MODEL), F32)] * 2
    return pl.pallas_call(
        functools.partial(_mixc_body, RT=RT, C=C, lseg=lseg, nt=nt, layer=layer, has_init=has_init,
                          n_prev=len(xa), fused=fused),
        grid=grid, in_specs=in_specs, out_specs=out_specs, out_shape=out_shape,
        input_output_aliases=aliases, scratch_shapes=scratch,
        compiler_params=_cparams(len(grid)),
        name="mix_hgrn",
    )(*args)


def _post_body(*refs, final):
    if final:
        (h_ref, a_ref, b_ref, c_ref, wo_ref, nw_ref, wg_ref, wu_ref, wd_ref, nf_ref, hout_ref, y_ref) = refs
    else:
        (h_ref, a_ref, b_ref, c_ref, wo_ref, nw_ref, wg_ref, wu_ref, wd_ref, hout_ref) = refs
    mix = (a_ref[...] + b_ref[...]) + c_ref[...]
    h1 = h_ref[...] + _dot(mix.astype(BF16), wo_ref[...])
    u = _rms(h1, nw_ref[...]).astype(BF16)
    act = (jax.nn.silu(_dot(u, wg_ref[...])) * _dot(u, wu_ref[...])).astype(BF16)
    h2 = h1 + _dot(act, wd_ref[...])
    hout_ref[...] = h2
    if final:
        y_ref[...] = _rms(h2, nf_ref[...])


def _post(h, oa, ob, oc, P, layer, tm, final):
    n = h.shape[0]
    row = pl.BlockSpec((tm, D_MODEL), lambda i: (i, 0))

    def wspec(shape):
        return pl.BlockSpec((None,) + shape, lambda i: (layer, 0, 0), pipeline_mode=pl.Buffered(1))

    in_specs = [row, row, row, row,
                wspec((D_MODEL, D_MODEL)), wspec((1, D_MODEL)),
                wspec((D_MODEL, D_FF)), wspec((D_MODEL, D_FF)), wspec((D_FF, D_MODEL))]
    args = [h, oa, ob, oc, P["w_out"], P["norm_ffn_w"], P["w_gate"], P["w_up"], P["w_down"]]
    out_specs = [row]
    out_shape = [jax.ShapeDtypeStruct((n, D_MODEL), F32)]
    if final:
        in_specs.append(pl.BlockSpec((1, D_MODEL), lambda i: (0, 0)))
        args.append(P["norm_final_w"])
        out_specs.append(row)
        out_shape.append(jax.ShapeDtypeStruct((n, D_MODEL), F32))
    return pl.pallas_call(
        functools.partial(_post_body, final=final),
        grid=(n // tm,),
        in_specs=in_specs, out_specs=out_specs, out_shape=out_shape,
        compiler_params=_cparams(1),
        name="post",
    )(*args)


def _trunk_cfg(B, L):
    n = B * L
    if L >= 512:
        return dict(fused=True, tm=min(n, 1024), NB=1, TA=256, TB=256, RC=256, CC=64, LS=64, tp=min(n, 512))
    nbc = min(B, 64 // L)
    return dict(fused=False, tm=min(n, 512), NB=min(B, 8), TA=L, TB=L, RC=nbc * L, CC=nbc * L, LS=L,
                tp=min(n, 512))


def _run_trunk(x, pos0, init, P):
    B, L, _ = x.shape
    cfg = _trunk_cfg(B, L)
    fused = cfg["fused"]
    n = B * L
    h = x.reshape(n, D_MODEL).astype(F32)
    rtab = _ret_tables(cfg["TB"], L, pos0)
    htab = _hg_tables(cfg["CC"], cfg["LS"])
    conv = lru = ret = hg = None
    y = None
    for l in range(DEPTH):
        if fused:
            src2, src3 = h, h.reshape(B, L, D_MODEL)
        else:
            src2 = _proj(h, P["norm_mix_w"], P["w_in"], l, cfg["tm"], D_MODEL)
            src3 = src2.reshape(B, L, IN_COLS)
        ia = None if init is None else (init[0], init[1])
        pa = None if l == 0 else (conv, lru)
        oa, conv, lru = _mixa(src3, P, l, ia, pa, B, L, cfg["TA"], cfg["NB"], pos0, fused)
        ob, ret = _mixb(src3, P, rtab, l, None if init is None else init[2], ret, B, L, cfg["TB"], cfg["NB"], fused)
        oc, hg = _mixc(src2, P, htab, l, None if init is None else init[3], hg, B, L, cfg["RC"], cfg["CC"],
                       cfg["LS"], fused)
        res = _post(h, oa.reshape(n, D_MODEL), ob.reshape(n, D_MODEL), oc, P, l, cfg["tp"], l == DEPTH - 1)
        h = res[0]
        if l == DEPTH - 1:
            y = res[1]
    return y.reshape(B, L, D_MODEL), conv, lru.reshape(DEPTH, B, D_MODEL), ret, hg


def kernel(x_prompt, x_sample, state_conv, state_lru, state_ret, state_hgrn, norm_mix_w, w_in, conv_w, conv_b, lru_wa, lru_ba, lru_wx, lru_bx, lru_lambda, hg_lb, hg_norm_w, w_out, norm_ffn_w, w_gate, w_up, w_down, norm_final_w):
    row = lambda a: a.astype(F32).reshape(DEPTH, 1, D_MODEL)
    P = dict(
        norm_mix_w=row(norm_mix_w), w_in=w_in.astype(BF16),
        conv_w=conv_w.astype(F32), conv_b=row(conv_b),
        lru_wa=lru_wa.astype(BF16), lru_ba=row(lru_ba), lru_wx=lru_wx.astype(BF16), lru_bx=row(lru_bx),
        lru_lambda=row(lru_lambda), hg_lb=hg_lb.astype(F32), hg_norm_w=row(hg_norm_w),
        w_out=w_out.astype(BF16), norm_ffn_w=row(norm_ffn_w),
        w_gate=w_gate.astype(BF16), w_up=w_up.astype(BF16), w_down=w_down.astype(BF16),
        norm_final_w=norm_final_w.astype(F32).reshape(1, D_MODEL))
    init_s = (state_conv.astype(F32), state_lru.astype(F32).reshape(DEPTH, -1, 1, D_MODEL),
              state_ret.astype(F32), state_hgrn.astype(F32))
    y_p, conv_p, lru_p, ret_p, hg_p = _run_trunk(x_prompt, 0, None, P)
    y_s, conv_s, lru_s, ret_s, hg_s = _run_trunk(x_sample, PAST_LEN, init_s, P)
    return (y_p.astype(x_prompt.dtype), y_s.astype(x_sample.dtype),
            conv_p.astype(state_conv.dtype), lru_p.astype(state_lru.dtype),
            ret_p.astype(state_ret.dtype), hg_p.astype(state_hgrn.dtype),
            conv_s.astype(state_conv.dtype), lru_s.astype(state_lru.dtype),
            ret_s.astype(state_ret.dtype), hg_s.astype(state_hgrn.dtype))
```

```python
import functools

import numpy as np
import jax
import jax.numpy as jnp
from jax import lax
from jax.experimental import pallas as pl
from jax.experimental.pallas import tpu as pltpu

F32 = jnp.float32
BF16 = jnp.bfloat16

D_MODEL = 1024
DEPTH = 4
PAST_LEN = 16384
LRU_BLOCKS = 4
LRU_BW = D_MODEL // LRU_BLOCKS
CONV_W = 4
LRU_C = 8.0
RET_HEADS = 4
RET_DK = 256
RET_DV = 256
ROPE_BASE = 10000.0
HG_DK = 128
HG_HEADS = 8
HG_DV = 128
D_FF = 2816
EPS = 1e-6
GN_EPS = 1e-5
LOG2E = 1.4426950408889634
N_GROUPS = 13
IN_COLS = N_GROUPS * D_MODEL
G_AX, G_AY, G_RQ, G_RK, G_RV, G_RG, G_CQ, G_CF, G_CI, G_CG, G_MA, G_MB, G_MC = range(13)
GROUPS_A = (G_AX, G_AY, G_MA)
GROUPS_B = (G_RQ, G_RK, G_RV, G_RG, G_MB)
GROUPS_C = (G_CQ, G_CF, G_CI, G_CG, G_MC)

VMEM_LIMIT = 56 * 1024 * 1024
SUBLANES = 8
LANES = 128


def _cparams(n_axes):
    return pltpu.CompilerParams(dimension_semantics=("arbitrary",) * n_axes,
                                vmem_limit_bytes=VMEM_LIMIT)


def _rms(x, w):
    return x * lax.rsqrt(jnp.mean(x * x, axis=-1, keepdims=True) + EPS) * w


def _dot(a, b):
    return jnp.dot(a, b, preferred_element_type=F32)


def _dot_nt(a, b):
    return lax.dot_general(a, b, (((1,), (1,)), ((), ())), preferred_element_type=F32)


def _dot_tn(a, b):
    return lax.dot_general(a, b, (((0,), (0,)), ((), ())), preferred_element_type=F32)


def _stacked_state(prev, shape_tail, nb, layer, batch, bidx):
    nd = len(shape_tail)
    out_spec = pl.BlockSpec((None, nb) + shape_tail, lambda *g: (layer, bidx(*g)) + (0,) * nd)
    out_shape = jax.ShapeDtypeStruct((DEPTH, batch) + shape_tail, F32)
    if prev is None:
        return [], [], out_spec, out_shape
    return [pl.BlockSpec(memory_space=pl.ANY)], [prev], out_spec, out_shape


def _grid_maps(fused, n_batch_blocks, nt):
    if not fused:
        return (n_batch_blocks, nt), (lambda b, t: b), (lambda b, t: t), None, None
    n_steps = n_batch_blocks * nt
    nxt = lambda s: jnp.minimum(s + 1, n_steps - 1)
    return ((n_steps,), (lambda s: lax.div(s, nt)), (lambda s: lax.rem(s, nt)),
            (lambda s: lax.div(nxt(s), nt)), (lambda s: lax.rem(nxt(s), nt)))


def _wspecs(layer, groups):
    return [pl.BlockSpec((None, D_MODEL, D_MODEL), lambda *_, g=g: (layer, 0, g), pipeline_mode=pl.Buffered(1))
            for g in groups]


def _project(h, nw_ref, w_refs, dst):
    u = _rms(h, nw_ref[...]).astype(BF16)
    for j, w_ref in enumerate(w_refs):
        dst[:, j * D_MODEL:(j + 1) * D_MODEL] = _dot(u, w_ref[...])


def _project_pieces(h, nw_ref, w_refs, dst, width=256):
    u = _rms(h, nw_ref[...]).astype(BF16)

    def piece(j, c):
        def run():
            dst[:, j * D_MODEL + c:j * D_MODEL + c + width] = _dot(u, w_refs[j][:, c:c + width])
        return run

    return [piece(j, c) for j in range(len(w_refs)) for c in range(0, D_MODEL, width)]


def _fused_run(h_ref, hn_ref, nw_ref, w_refs, z_bufs, squeeze, main):
    s = pl.program_id(0)
    rd = (lambda r: r[0]) if squeeze else (lambda r: r[...])

    @pl.when(s == 0)
    def _():
        _project(rd(h_ref), nw_ref, w_refs, z_bufs[0])

    for parity in range(2):
        @pl.when(lax.rem(s, 2) == parity)
        def _():
            pieces = _project_pieces(rd(hn_ref), nw_ref, w_refs, z_bufs[1 - parity])

            def emit(k=None):
                for _ in range(len(pieces) if k is None else min(k, len(pieces))):
                    pieces.pop(0)()

            main(z_bufs[parity], emit)
            emit()


def _proj_body(h_ref, nw_ref, w_ref, z_ref, u_scr):
    @pl.when(pl.program_id(1) == 0)
    def _():
        u_scr[...] = _rms(h_ref[...], nw_ref[...]).astype(BF16)

    z_ref[...] = _dot(u_scr[...], w_ref[...])


def _proj(h, nw_all, w_all, layer, tm, tn):
    n = h.shape[0]
    ncols = w_all.shape[2]
    return pl.pallas_call(
        _proj_body,
        grid=(n // tm, ncols // tn),
        in_specs=[
            pl.BlockSpec((tm, D_MODEL), lambda i, j: (i, 0)),
            pl.BlockSpec((None, 1, D_MODEL), lambda i, j: (layer, 0, 0)),
            pl.BlockSpec((None, D_MODEL, tn), lambda i, j: (layer, 0, j)),
        ],
        out_specs=pl.BlockSpec((tm, tn), lambda i, j: (i, j)),
        out_shape=jax.ShapeDtypeStruct((n, ncols), F32),
        scratch_shapes=[pltpu.VMEM((tm, D_MODEL), BF16)],
        compiler_params=_cparams(2),
        name="proj",
    )(h, nw_all, w_all)


def _mixa_body(*refs, T, NB, nt, pos0, has_init, n_prev, fused):
    n_in = 6 if fused else 3
    (cw_ref, cb_ref, wa_ref, ba_ref, wx_ref, bx_ref, lam_ref) = refs[n_in:n_in + 7]
    k = n_in + 7
    if has_init:
        conv0_ref, lru0_ref = refs[k:k + 2]
        k += 2
    k += n_prev
    out_ref, conv_out_ref, lru_out_ref, xpad, hcar, a_scr, u_scr, h_scr, g_scr = refs[k:k + 9]
    t = lax.rem(pl.program_id(0), nt) if fused else pl.program_id(1)

    @pl.when(t == 0)
    def _():
        if has_init:
            xpad[:, 5:8, :] = conv0_ref[...]
            hcar[...] = lru0_ref[...]
        else:
            xpad[:, 5:8, :] = jnp.zeros((NB, 3, D_MODEL), F32)
            hcar[...] = jnp.zeros((NB, 1, D_MODEL), F32)

    sp = jax.nn.softplus(-lam_ref[...])
    cw = cw_ref[...]
    cb = cb_ref[...]
    rowpos = pos0 + t * T + lax.broadcasted_iota(jnp.int32, (T, 1), 0)

    RB = min(T, 64)

    def gates_and_inputs(xc, rows, zy, zm):
        xcb = xc.astype(BF16)
        ra = jnp.concatenate(
            [_dot(xcb[:, n * LRU_BW:(n + 1) * LRU_BW], wa_ref[n]) for n in range(LRU_BLOCKS)], axis=1)
        rx = jnp.concatenate(
            [_dot(xcb[:, n * LRU_BW:(n + 1) * LRU_BW], wx_ref[n]) for n in range(LRU_BLOCKS)], axis=1)
        r = jax.nn.sigmoid(ra + ba_ref[...])
        i = jax.nn.sigmoid(rx + bx_ref[...])
        log_a = -LRU_C * r * sp
        a = jnp.exp(log_a)
        th = jnp.tanh(log_a)
        mult = jnp.sqrt(-2.0 * th / (1.0 - th))
        mult = jnp.where(rowpos[rows] == 0, 1.0, mult)
        a_scr[rows, :] = a
        u_scr[rows, :] = xc * i * mult
        g_scr[rows, :] = jax.nn.sigmoid(zm) * jax.nn.gelu(zy)

    def one(ib, col, emit):
        xp = xpad.at[ib]
        if T >= SUBLANES:
            prev = xp[0:SUBLANES, :]
            for rb in range(T // RB):
                rows = slice(rb * RB, (rb + 1) * RB)
                x = col(ib, 0, rows)
                xcat = jnp.concatenate([prev, x], axis=0)
                xc = cb + cw[0:1] * pltpu.roll(xcat, 3, 0)[SUBLANES:]
                xc = xc + cw[1:2] * pltpu.roll(xcat, 2, 0)[SUBLANES:]
                xc = xc + cw[2:3] * pltpu.roll(xcat, 1, 0)[SUBLANES:]
                xc = xc + cw[3:4] * x
                prev = x[RB - SUBLANES:, :]
                gates_and_inputs(xc, rows, col(ib, 1, rows), col(ib, 2, rows))
                nblk, npc = T // RB, 4 * len(GROUPS_A)
                emit(npc * (rb + 1) // nblk - npc * rb // nblk)
            xp[0:SUBLANES, :] = prev
            conv_out_ref[ib] = prev[5:8, :]
        else:
            rows = slice(0, T)
            x = col(ib, 0, rows)
            xp[8:8 + T, :] = x
            xc = cb + cw[0:1] * xp[5:5 + T, :]
            xc = xc + cw[1:2] * xp[6:6 + T, :]
            xc = xc + cw[2:3] * xp[7:7 + T, :]
            xc = xc + cw[3:4] * x
            tail = xp[T + 5:T + 8, :]
            xp[5:8, :] = tail
            conv_out_ref[ib] = tail
            gates_and_inputs(xc, rows, col(ib, 1, rows), col(ib, 2, rows))

        def step(s, h):
            h = a_scr[pl.ds(s, 1), :] * h + u_scr[pl.ds(s, 1), :]
            h_scr[pl.ds(s, 1), :] = h
            return h

        h_last = lax.fori_loop(0, T, step, hcar[ib], unroll=min(T, 8))
        hcar[ib] = h_last
        lru_out_ref[ib] = h_last
        out_ref[ib] = g_scr[...] * h_scr[...]

    def main(zc, emit):
        if fused:
            col = lambda ib, j, rows: zc[rows, j * D_MODEL:(j + 1) * D_MODEL]
        else:
            col = lambda ib, j, rows: refs[j][ib, rows, :]
            emit = lambda k: None
        if NB == 1:
            one(0, col, emit)
        else:
            def body(ib, c):
                one(ib, col, emit)
                return c
            lax.fori_loop(0, NB, body, 0)

    if fused:
        _fused_run(refs[0], refs[1], refs[2], refs[3:6], refs[k + 9:k + 11], True, main)
    else:
        main(None, None)


def _mixa(src, P, layer, init, prev, B, L, T, NB, pos0, fused):
    has_init = init is not None
    nt = L // T
    grid, bidx, tidx, bnxt, tnxt = _grid_maps(fused, B // NB, nt)
    blk = lambda bf, tf, g=0: pl.BlockSpec((NB, T, D_MODEL), lambda *a: (bf(*a), tf(*a), g))
    pspec = lambda shape: pl.BlockSpec((None,) + shape, lambda *a: (layer,) + (0,) * len(shape))

    if fused:
        in_specs = [blk(bidx, tidx), blk(bnxt, tnxt), pspec((1, D_MODEL))] + _wspecs(layer, GROUPS_A)
        args = [src, src, P["norm_mix_w"]] + [P["w_in"]] * 3
    else:
        in_specs = [blk(bidx, tidx, g) for g in GROUPS_A]
        args = [src] * 3
    in_specs += [pspec((CONV_W, D_MODEL)), pspec((1, D_MODEL)),
                 pspec((LRU_BLOCKS, LRU_BW, LRU_BW)), pspec((1, D_MODEL)),
                 pspec((LRU_BLOCKS, LRU_BW, LRU_BW)), pspec((1, D_MODEL)), pspec((1, D_MODEL))]
    args += [P["conv_w"], P["conv_b"], P["lru_wa"], P["lru_ba"], P["lru_wx"], P["lru_bx"], P["lru_lambda"]]
    if has_init:
        conv0, lru0 = init
        in_specs += [pl.BlockSpec((None, NB, CONV_W - 1, D_MODEL), lambda *a: (layer, bidx(*a), 0, 0)),
                     pl.BlockSpec((None, NB, 1, D_MODEL), lambda *a: (layer, bidx(*a), 0, 0))]
        args += [conv0, lru0]
    pc, pl_ = (None, None) if prev is None else prev
    xs1, xa1, conv_spec, conv_shape = _stacked_state(pc, (CONV_W - 1, D_MODEL), NB, layer, B, bidx)
    xs2, xa2, lru_spec, lru_shape = _stacked_state(pl_, (1, D_MODEL), NB, layer, B, bidx)
    aliases = {} if prev is None else {len(args): 1, len(args) + 1: 2}
    in_specs += xs1 + xs2
    args += xa1 + xa2
    out_specs = [blk(bidx, tidx), conv_spec, lru_spec]
    out_shape = [jax.ShapeDtypeStruct((B, L, D_MODEL), F32), conv_shape, lru_shape]
    scratch = [pltpu.VMEM((NB, T + 8, D_MODEL), F32), pltpu.VMEM((NB, 1, D_MODEL), F32),
               pltpu.VMEM((T, D_MODEL), F32), pltpu.VMEM((T, D_MODEL), F32), pltpu.VMEM((T, D_MODEL), F32),
               pltpu.VMEM((T, D_MODEL), F32)]
    if fused:
        scratch += [pltpu.VMEM((T, len(GROUPS_A) * D_MODEL), F32)] * 2
    return pl.pallas_call(
        functools.partial(_mixa_body, T=T, NB=NB, nt=nt, pos0=pos0, has_init=has_init,
                          n_prev=len(xa1 + xa2), fused=fused),
        grid=grid, in_specs=in_specs, out_specs=out_specs, out_shape=out_shape,
        input_output_aliases=aliases, scratch_shapes=scratch,
        compiler_params=_cparams(len(grid)),
        name="mix_lru",
    )(*args)


def _mixb_body(*refs, NB, nt, has_init, n_prev, fused):
    n_in = 8 if fused else 5
    (cos_ref, sin_ref, d_ref, xi_ref, zeta_ref, gc_ref) = refs[n_in:n_in + 6]
    k = n_in + 6
    if has_init:
        r0_ref = refs[k]
        k += 1
    k += n_prev
    out_ref, r_out_ref, r_scr = refs[k:k + 3]
    t = lax.rem(pl.program_id(0), nt) if fused else pl.program_id(1)
    direct = has_init and nt == 1

    if not direct:
        @pl.when(t == 0)
        def _():
            if has_init:
                r_scr[...] = r0_ref[...]
            else:
                r_scr[...] = jnp.zeros(r_scr.shape, F32)

    cos = cos_ref[...]
    sin = sin_ref[...]
    half = RET_DK // 2

    def rot(x):
        x1, x2 = x[:, :half], x[:, half:]
        return jnp.concatenate([x1 * cos - x2 * sin, x1 * sin + x2 * cos], axis=1)

    def one(ib, col, emit):
        rr = r0_ref.at[ib] if direct else r_scr.at[ib]
        rw = r_out_ref.at[ib] if direct else rr
        for h in range(RET_HEADS):
            sl = slice(h * RET_DK, (h + 1) * RET_DK)
            qr = rot(col(ib, 0, sl))
            kr = rot(col(ib, 1, sl)) * RET_DK ** -0.5
            qb = qr.astype(BF16)
            vb = col(ib, 2, sl).astype(BF16)
            s = _dot_nt(qb, kr.astype(BF16)) * d_ref[h]
            emit(1)
            rh = rr[h]
            o = _dot(s.astype(BF16), vb) + _dot(qb, rh.astype(BF16)) * xi_ref[:, sl]
            emit(1)
            kz = (kr * zeta_ref[:, sl]).astype(BF16)
            rw[h] = gc_ref[:, sl] * rh + _dot_tn(kz, vb)
            emit(1)
            d = o - jnp.mean(o, axis=-1, keepdims=True)
            gn = d * lax.rsqrt(jnp.mean(d * d, axis=-1, keepdims=True) + GN_EPS)
            emit(1)
            out_ref[ib, :, sl] = jax.nn.sigmoid(col(ib, 4, sl)) * (gn * jax.nn.silu(col(ib, 3, sl)))
            emit(1)
        if not direct:
            r_out_ref[ib] = rr[...]

    def main(zc, emit):
        if fused:
            col = lambda ib, j, sl: zc[:, j * D_MODEL + sl.start:j * D_MODEL + sl.stop]
        else:
            col = lambda ib, j, sl: refs[j][ib, :, sl]
            emit = lambda k: None
        if NB == 1:
            one(0, col, emit)
        else:
            def body(ib, c):
                one(ib, col, emit)
                return c
            lax.fori_loop(0, NB, body, 0)

    if fused:
        _fused_run(refs[0], refs[1], refs[2], refs[3:8], refs[k + 3:k + 5], True, main)
    else:
        main(None, None)


def _ret_tables(C, L, pos0):
    t = jnp.arange(C, dtype=F32)
    log_g = jnp.log1p(-jnp.exp2(-5.0 - jnp.arange(RET_HEADS, dtype=F32)))
    rel = t[:, None] - t[None, :]
    dmat = jnp.exp(jnp.where(rel[None] >= 0, rel[None] * log_g[:, None, None], -jnp.inf))
    xi = jnp.exp((t[None, :] + 1.0) * log_g[:, None]).T
    zeta = jnp.exp((C - 1.0 - t[None, :]) * log_g[:, None]).T
    g_c = jnp.exp(C * log_g)
    rep = lambda a: jnp.repeat(a, RET_DV, axis=-1)
    half = RET_DK // 2
    pos = pos0 + jnp.arange(L, dtype=jnp.int32)
    inv = ROPE_BASE ** (-jnp.arange(half, dtype=F32) / half)
    ang = pos.astype(F32)[:, None] * inv[None, :]
    return dict(d=dmat, xi=rep(xi), zeta=rep(zeta), gc=rep(g_c[None, :]), cos=jnp.cos(ang), sin=jnp.sin(ang))


def _mixb(src, P, tab, layer, init, prev, B, L, T, NB, fused):
    has_init = init is not None
    nt = L // T
    grid, bidx, tidx, bnxt, tnxt = _grid_maps(fused, B // NB, nt)
    blk = lambda bf, tf, g=0: pl.BlockSpec((NB, T, D_MODEL), lambda *a: (bf(*a), tf(*a), g))
    const = lambda shape: pl.BlockSpec(shape, lambda *a: (0,) * len(shape))

    if fused:
        in_specs = ([blk(bidx, tidx), blk(bnxt, tnxt), pl.BlockSpec((None, 1, D_MODEL), lambda *a: (layer, 0, 0))]
                    + _wspecs(layer, GROUPS_B))
        args = [src, src, P["norm_mix_w"]] + [P["w_in"]] * 5
    else:
        in_specs = [blk(bidx, tidx, g) for g in GROUPS_B]
        args = [src] * 5
    in_specs += [pl.BlockSpec((T, RET_DK // 2), lambda *a: (tidx(*a), 0)),
                 pl.BlockSpec((T, RET_DK // 2), lambda *a: (tidx(*a), 0)),
                 const((RET_HEADS, T, T)), const((T, D_MODEL)), const((T, D_MODEL)), const((1, D_MODEL))]
    args += [tab["cos"], tab["sin"], tab["d"], tab["xi"], tab["zeta"], tab["gc"]]
    st_shape = (RET_HEADS, RET_DK, RET_DV)
    if has_init:
        in_specs.append(pl.BlockSpec((None, NB) + st_shape, lambda *a: (layer, bidx(*a), 0, 0, 0)))
        args.append(init)
    xs, xa, st_spec, st_out = _stacked_state(prev, st_shape, NB, layer, B, bidx)
    aliases = {} if prev is None else {len(args): 1}
    in_specs += xs
    args += xa
    out_specs = [blk(bidx, tidx), st_spec]
    out_shape = [jax.ShapeDtypeStruct((B, L, D_MODEL), F32), st_out]
    scratch = [pltpu.VMEM(((1 if has_init and nt == 1 else NB),) + st_shape, F32)]
    if fused:
        scratch += [pltpu.VMEM((T, len(GROUPS_B) * D_MODEL), F32)] * 2
    return pl.pallas_call(
        functools.partial(_mixb_body, NB=NB, nt=nt, has_init=has_init, n_prev=len(xa), fused=fused),
        grid=grid, in_specs=in_specs, out_specs=out_specs, out_shape=out_shape,
        input_output_aliases=aliases, scratch_shapes=scratch,
        compiler_params=_cparams(len(grid)),
        name="mix_ret",
    )(*args)


def _hg_levels(lseg):
    ms = []
    m = lseg // 2
    while m >= 1:
        ms.append(m)
        m //= 2
    return ms


def _hg_tables(C, lseg):
    t = np.arange(C)
    masks, sgn = [], []
    for m in _hg_levels(lseg):
        blk = 2 * m
        same = (t[:, None] // blk) == (t[None, :] // blk)
        masks.append(same & ((t[:, None] % blk) >= m) & ((t[None, :] % blk) < m))
        if m < SUBLANES:
            sgn.append(np.where((t % blk) >= m, 1.0, -1.0)[:, None] * np.ones((1, LANES)))
    masks.append(np.eye(C, dtype=bool))
    masks = np.stack(masks).astype(np.float32)
    masks2 = np.concatenate([masks, masks], axis=2)
    seg = t // lseg
    ltri = ((seg[:, None] == seg[None, :]) & (t[None, :] <= t[:, None])).astype(np.float32)
    sgn = np.stack(sgn).astype(np.float32) if sgn else np.ones((1, C, LANES), np.float32)
    return dict(masks=jnp.asarray(masks2), ltri=jnp.asarray(ltri).astype(BF16), sgn=jnp.asarray(sgn))


def _row_bcast(ref, C, blk, off):
    W = ref.shape[1]
    if blk >= SUBLANES:
        pieces = [jnp.broadcast_to(ref[j * blk + off:j * blk + off + 1, :], (blk, W)) for j in range(C // blk)]
    else:
        rowi = lax.broadcasted_iota(jnp.int32, (SUBLANES, W), 0)
        pieces = []
        for g in range(C // SUBLANES):
            acc = None
            for j in range(SUBLANES // blk):
                r = g * SUBLANES + j * blk + off
                bc = jnp.broadcast_to(ref[r:r + 1, :], (SUBLANES, W))
                acc = bc if acc is None else jnp.where(rowi >= j * blk, bc, acc)
            pieces.append(acc)
    return pieces[0] if len(pieces) == 1 else jnp.concatenate(pieces, axis=0)


def _mixc_body(*refs, RT, C, lseg, nt, layer, has_init, n_prev, fused):
    n_in = 8 if fused else 5
    (lb_ref, nw_ref, ltri_ref, mask_ref, sgn_ref) = refs[n_in:n_in + 5]
    k = n_in + 5
    if has_init:
        s0_ref = refs[k]
        k += 1
    k += n_prev
    out_ref, s_out_ref, st_scr, b_scr, qe_scr, ke_scr, oi_scr, bh_scr = refs[k:k + 8]
    t = lax.rem(pl.program_id(0), nt) if fused else pl.program_id(1)
    NS = C // lseg
    NP = HG_HEADS // 2
    PW = 2 * HG_DK

    @pl.when(t == 0)
    def _():
        if has_init:
            st_scr[...] = s0_ref[...]
        else:
            st_scr[...] = jnp.zeros(st_scr.shape, F32)

    if layer > 0:
        x = lb_ref[...]
        e = jnp.exp(x - jnp.max(x, axis=0, keepdims=True))
        sm = e / jnp.sum(e, axis=0, keepdims=True)
        lb = sm[1:2]
        for j in range(2, layer + 1):
            lb = lb + sm[j:j + 1]
    levels = _hg_levels(lseg)

    def blockdiag_rows(pair):
        z = jnp.zeros((pair.shape[0], HG_DK), pair.dtype)
        return jnp.concatenate([jnp.concatenate([pair[:, :HG_DK], z], axis=1),
                                jnp.concatenate([z, pair[:, HG_DK:]], axis=1)], axis=0)

    def front(r0, col, colh, emit):
        rows = pl.ds(r0, C)
        cf = col(1, rows)
        if layer == 0:
            logf = jax.nn.log_sigmoid(cf)
            kk = jax.nn.sigmoid(-cf)
        else:
            sg = jax.nn.sigmoid(cf)
            logf = jnp.log(lb + (1.0 - lb) * sg)
            kk = (1.0 - lb) * (1.0 - sg)
        qq = jax.nn.silu(col(0, rows))
        vb = col(2, rows).astype(BF16)
        hi = logf.astype(BF16)
        r1 = logf - hi.astype(F32)
        mid = r1.astype(BF16)
        lo = (r1 - mid.astype(F32)).astype(BF16)
        lt = ltri_ref[...]
        b2 = ((_dot(lt, lo) + _dot(lt, mid)) + _dot(lt, hi)) * LOG2E
        b_scr[...] = b2
        bend = _row_bcast(b_scr, C, lseg, lseg - 1)
        qe = qq * jnp.exp2(b2)
        ke = kk * jnp.exp2(bend - b2)
        qb = qq.astype(BF16)
        kb = kk.astype(BF16)
        emit(1)

        def level_w(li, m):
            blk = 2 * m
            if m >= SUBLANES:
                pieces = []
                for j in range(C // blk):
                    lo = slice(j * blk, j * blk + m)
                    up = slice(j * blk + m, (j + 1) * blk)
                    brow = b_scr[j * blk + m - 1:j * blk + m, :]
                    pieces.append(kk[lo] * jnp.exp2(brow - b2[lo]))
                    pieces.append(qq[up] * jnp.exp2(b2[up] - brow))
                w = jnp.concatenate(pieces, axis=0)
            else:
                si = li - (len(levels) - sgn_ref.shape[0])
                sgn = jnp.concatenate([sgn_ref[si]] * (D_MODEL // LANES), axis=1)
                e = jnp.exp2((b2 - _row_bcast(b_scr, C, blk, m - 1)) * sgn)
                w = jnp.where(sgn > 0.0, qq, kk) * e
            return w.astype(BF16)

        amat = [mask_ref[len(levels)] * _dot_nt(qb[:, p * PW:(p + 1) * PW], blockdiag_rows(kb[:, p * PW:(p + 1) * PW]))
                for p in range(NP)]
        for li, m in enumerate(levels):
            w = level_w(li, m)
            for p in range(NP):
                wp = w[:, p * PW:(p + 1) * PW]
                amat[p] = amat[p] + mask_ref[li] * _dot_nt(wp, blockdiag_rows(wp))
            if li % 2 == 0:
                emit(1)
        bcol = b_scr[C - SUBLANES:C, :] if NS == 1 else None
        return dict(r0=r0, rows=rows, amat=amat, vb=vb, qe=qe, ke=ke, b2=b2, bcol=bcol)

    def back(v, col, colh, emit):
        r0, rows, amat, vb, qe, ke, b2, bcol = (v[k] for k in ("r0", "rows", "amat", "vb", "qe", "ke", "b2", "bcol"))
        o_intra = jnp.concatenate(
            [_dot(amat[p].astype(BF16), blockdiag_rows(vb[:, p * PW:(p + 1) * PW])) for p in range(NP)], axis=1)

        if NS == 1:
            o_parts = []
            for p in range(NP):
                sbd = blockdiag_rows(jnp.concatenate([st_scr[0, 2 * p], st_scr[0, 2 * p + 1]], axis=1))
                o_parts.append(_dot(qe[:, p * PW:(p + 1) * PW].astype(BF16), sbd.astype(BF16)))
            o_inter = jnp.concatenate(o_parts, axis=1)
            emit(1)
            keb = ke.astype(BF16)
            for h in range(HG_HEADS):
                sl = slice(h * HG_DK, (h + 1) * HG_DK)
                dec = jnp.exp2(bcol[:, sl].T[:, SUBLANES - 1:SUBLANES])
                st_scr[0, h] = dec * st_scr[0, h] + _dot_tn(keb[:, sl], vb[:, sl])
        else:
            qe_scr[...] = qe
            ke_scr[...] = ke
            for h in range(HG_HEADS):
                sl = slice(h * HG_DK, (h + 1) * HG_DK)
                bh_scr[h] = b2[:, sl]
                bends = bh_scr[h, pl.ds(lseg - 1, NS, stride=lseg), :]
                dec_all = jnp.exp2(bends.T)
                for s in range(NS):
                    rs = slice(s * lseg, (s + 1) * lseg)
                    st = st_scr[s, h]
                    oi_scr[rs, sl] = _dot(qe_scr[rs, sl].astype(BF16), st.astype(BF16))
                    st_scr[s, h] = dec_all[:, s:s + 1] * st + _dot_tn(
                        ke_scr[rs, sl].astype(BF16), colh(2, pl.ds(r0 + s * lseg, lseg), sl).astype(BF16))
            o_inter = oi_scr[...]

        o = o_inter + o_intra
        gate = jax.nn.sigmoid(col(4, rows))
        og = nw_ref[...] * jax.nn.silu(col(3, rows))
        for h in range(HG_HEADS):
            sl = slice(h * HG_DK, (h + 1) * HG_DK)
            oh = o[:, sl]
            rms = oh * lax.rsqrt(jnp.mean(oh * oh, axis=-1, keepdims=True) + EPS)
            out_ref[rows, sl] = gate[:, sl] * (rms * og[:, sl])

    def chunk(r0, col, colh, emit):
        back(front(r0, col, colh, emit), col, colh, emit)

    def main(zc, emit):
        if fused:
            col = lambda j, rows: zc[rows, j * D_MODEL:(j + 1) * D_MODEL]
            colh = lambda j, rows, sl: zc[rows, j * D_MODEL + sl.start:j * D_MODEL + sl.stop]
            nch = RT // C
            v = front(0, col, colh, emit)
            for c in range(nch):
                vn = front((c + 1) * C, col, colh, emit) if c + 1 < nch else None
                back(v, col, colh, emit)
                v = vn
            return
        col = lambda j, rows: refs[j][rows, :]
        colh = lambda j, rows, sl: refs[j][rows, sl]
        emit = lambda k: None
        if RT == C:
            chunk(0, col, colh, emit)
        else:
            def cbody(c, carry):
                chunk(pl.multiple_of(c * C, C), col, colh, emit)
                return carry
            lax.fori_loop(0, RT // C, cbody, 0)

    if fused:
        _fused_run(refs[0], refs[1], refs[2], refs[3:8], refs[k + 8:k + 10], False, main)
    else:
        main(None, None)

    s_out_ref[...] = st_scr[...]


def _mixc(src, P, tab, layer, init, prev, B, L, RT, C, lseg, fused):
    n = B * L
    has_init = init is not None
    NS = C // lseg
    nt = L // RT if lseg == C else 1
    nlev = len(_hg_levels(lseg)) + 1
    grid, bidx, tidx, bnxt, tnxt = _grid_maps(fused, n // (RT * nt), nt)
    blk = lambda bf, tf, g=0: pl.BlockSpec((RT, D_MODEL), lambda *a: (bf(*a) * nt + tf(*a), g))
    const = lambda shape: pl.BlockSpec(shape, lambda *a: (0,) * len(shape))

    if fused:
        in_specs = ([blk(bidx, tidx), blk(bnxt, tnxt), pl.BlockSpec((None, 1, D_MODEL), lambda *a: (layer, 0, 0))]
                    + _wspecs(layer, GROUPS_C))
        args = [src, src, P["norm_mix_w"]] + [P["w_in"]] * 5
    else:
        in_specs = [blk(bidx, tidx, g) for g in GROUPS_C]
        args = [src] * 5
    in_specs += [const((DEPTH, D_MODEL)), pl.BlockSpec((None, 1, D_MODEL), lambda *a: (layer, 0, 0)),
                 const((C, C)), const((nlev, C, 2 * C)), const(tab["sgn"].shape)]
    args += [P["hg_lb"], P["hg_norm_w"], tab["ltri"], tab["masks"], tab["sgn"]]
    st_shape = (HG_HEADS, HG_DK, HG_DV)
    if has_init:
        in_specs.append(pl.BlockSpec((None, NS) + st_shape, lambda *a: (layer, bidx(*a), 0, 0, 0)))
        args.append(init)
    xs, xa, st_spec, st_out = _stacked_state(prev, st_shape, NS, layer, B, bidx)
    aliases = {} if prev is None else {len(args): 1}
    in_specs += xs
    args += xa
    out_specs = [blk(bidx, tidx), st_spec]
    out_shape = [jax.ShapeDtypeStruct((n, D_MODEL), F32), st_out]
    scratch = [pltpu.VMEM((NS,) + st_shape, F32), pltpu.VMEM((C, D_MODEL), F32),
               pltpu.VMEM((C, D_MODEL), F32), pltpu.VMEM((C, D_MODEL), F32),
               pltpu.VMEM((C, D_MODEL), F32), pltpu.VMEM((HG_HEADS, C, HG_DK), F32)]
    if fused:
        scratch += [pltpu.VMEM((RT, len(GROUPS_C) * D_MODEL), F32)] * 2
    return pl.pallas_call(
        functools.partial(_mixc_body, RT=RT, C=C, lseg=lseg, nt=nt, layer=layer, has_init=has_init,
                          n_prev=len(xa), fused=fused),
        grid=grid, in_specs=in_specs, out_specs=out_specs, out_shape=out_shape,
        input_output_aliases=aliases, scratch_shapes=scratch,
        compiler_params=_cparams(len(grid)),
        name="mix_hgrn",
    )(*args)


def _post_body(*refs, final):
    if final:
        (h_ref, a_ref, b_ref, c_ref, wo_ref, nw_ref, wg_ref, wu_ref, wd_ref, nf_ref, hout_ref, y_ref) = refs
    else:
        (h_ref, a_ref, b_ref, c_ref, wo_ref, nw_ref, wg_ref, wu_ref, wd_ref, hout_ref) = refs
    mix = (a_ref[...] + b_ref[...]) + c_ref[...]
    h1 = h_ref[...] + _dot(mix.astype(BF16), wo_ref[...])
    u = _rms(h1, nw_ref[...]).astype(BF16)
    act = (jax.nn.silu(_dot(u, wg_ref[...])) * _dot(u, wu_ref[...])).astype(BF16)
    h2 = h1 + _dot(act, wd_ref[...])
    hout_ref[...] = h2
    if final:
        y_ref[...] = _rms(h2, nf_ref[...])


def _post(h, oa, ob, oc, P, layer, tm, final):
    n = h.shape[0]
    row = pl.BlockSpec((tm, D_MODEL), lambda i: (i, 0))

    def wspec(shape):
        return pl.BlockSpec((None,) + shape, lambda i: (layer, 0, 0), pipeline_mode=pl.Buffered(1))

    in_specs = [row, row, row, row,
                wspec((D_MODEL, D_MODEL)), wspec((1, D_MODEL)),
                wspec((D_MODEL, D_FF)), wspec((D_MODEL, D_FF)), wspec((D_FF, D_MODEL))]
    args = [h, oa, ob, oc, P["w_out"], P["norm_ffn_w"], P["w_gate"], P["w_up"], P["w_down"]]
    out_specs = [row]
    out_shape = [jax.ShapeDtypeStruct((n, D_MODEL), F32)]
    if final:
        in_specs.append(pl.BlockSpec((1, D_MODEL), lambda i: (0, 0)))
        args.append(P["norm_final_w"])
        out_specs.append(row)
        out_shape.append(jax.ShapeDtypeStruct((n, D_MODEL), F32))
    return pl.pallas_call(
        functools.partial(_post_body, final=final),
        grid=(n // tm,),
        in_specs=in_specs, out_specs=out_specs, out_shape=out_shape,
        compiler_params=_cparams(1),
        name="post",
    )(*args)


def _trunk_cfg(B, L):
    n = B * L
    if L >= 512:
        return dict(fused=True, tm=min(n, 1024), NB=1, TA=256, TB=256, RC=256, CC=64, LS=64, tp=min(n, 512))
    nbc = min(B, 64 // L)
    return dict(fused=False, tm=min(n, 512), NB=min(B, 8), TA=L, TB=L, RC=nbc * L, CC=nbc * L, LS=L,
                tp=min(n, 512))


def _run_trunk(x, pos0, init, P):
    B, L, _ = x.shape
    cfg = _trunk_cfg(B, L)
    fused = cfg["fused"]
    n = B * L
    h = x.reshape(n, D_MODEL).astype(F32)
    rtab = _ret_tables(cfg["TB"], L, pos0)
    htab = _hg_tables(cfg["CC"], cfg["LS"])
    conv = lru = ret = hg = None
    y = None
    for l in range(DEPTH):
        if fused:
            src2, src3 = h, h.reshape(B, L, D_MODEL)
        else:
            src2 = _proj(h, P["norm_mix_w"], P["w_in"], l, cfg["tm"], IN_COLS // 4)
            src3 = src2.reshape(B, L, IN_COLS)
        ia = None if init is None else (init[0], init[1])
        pa = None if l == 0 else (conv, lru)
        oa, conv, lru = _mixa(src3, P, l, ia, pa, B, L, cfg["TA"], cfg["NB"], pos0, fused)
        ob, ret = _mixb(src3, P, rtab, l, None if init is None else init[2], ret, B, L, cfg["TB"], cfg["NB"], fused)
        oc, hg = _mixc(src2, P, htab, l, None if init is None else init[3], hg, B, L, cfg["RC"], cfg["CC"],
                       cfg["LS"], fused)
        res = _post(h, oa.reshape(n, D_MODEL), ob.reshape(n, D_MODEL), oc, P, l, cfg["tp"], l == DEPTH - 1)
        h = res[0]
        if l == DEPTH - 1:
            y = res[1]
    return y.reshape(B, L, D_MODEL), conv, lru.reshape(DEPTH, B, D_MODEL), ret, hg


def kernel(x_prompt, x_sample, state_conv, state_lru, state_ret, state_hgrn, norm_mix_w, w_in, conv_w, conv_b, lru_wa, lru_ba, lru_wx, lru_bx, lru_lambda, hg_lb, hg_norm_w, w_out, norm_ffn_w, w_gate, w_up, w_down, norm_final_w):
    row = lambda a: a.astype(F32).reshape(DEPTH, 1, D_MODEL)
    P = dict(
        norm_mix_w=row(norm_mix_w), w_in=w_in.astype(BF16),
        conv_w=conv_w.astype(F32), conv_b=row(conv_b),
        lru_wa=lru_wa.astype(BF16), lru_ba=row(lru_ba), lru_wx=lru_wx.astype(BF16), lru_bx=row(lru_bx),
        lru_lambda=row(lru_lambda), hg_lb=hg_lb.astype(F32), hg_norm_w=row(hg_norm_w),
        w_out=w_out.astype(BF16), norm_ffn_w=row(norm_ffn_w),
        w_gate=w_gate.astype(BF16), w_up=w_up.astype(BF16), w_down=w_down.astype(BF16),
        norm_final_w=norm_final_w.astype(F32).reshape(1, D_MODEL))
    init_s = (state_conv.astype(F32), state_lru.astype(F32).reshape(DEPTH, -1, 1, D_MODEL),
              state_ret.astype(F32), state_hgrn.astype(F32))
    y_p, conv_p, lru_p, ret_p, hg_p = _run_trunk(x_prompt, 0, None, P)
    y_s, conv_s, lru_s, ret_s, hg_s = _run_trunk(x_sample, PAST_LEN, init_s, P)
    return (y_p.astype(x_prompt.dtype), y_s.astype(x_sample.dtype),
            conv_p.astype(state_conv.dtype), lru_p.astype(state_lru.dtype),
            ret_p.astype(state_ret.dtype), hg_p.astype(state_hgrn.dtype),
            conv_s.astype(state_conv.dtype), lru_s.astype(state_lru.dtype),
            ret_s.astype(state_ret.dtype), hg_s.astype(state_hgrn.dtype))
```

```python
import functools

import numpy as np
import jax
import jax.numpy as jnp
from jax import lax
from jax.experimental import pallas as pl
from jax.experimental.pallas import tpu as pltpu

F32 = jnp.float32
BF16 = jnp.bfloat16

D_MODEL = 1024
DEPTH = 4
PAST_LEN = 16384
LRU_BLOCKS = 4
LRU_BW = D_MODEL // LRU_BLOCKS
CONV_W = 4
LRU_C = 8.0
RET_HEADS = 4
RET_DK = 256
RET_DV = 256
ROPE_BASE = 10000.0
HG_DK = 128
HG_HEADS = 8
HG_DV = 128
D_FF = 2816
EPS = 1e-6
GN_EPS = 1e-5
LOG2E = 1.4426950408889634
N_GROUPS = 13
IN_COLS = N_GROUPS * D_MODEL
G_AX, G_AY, G_RQ, G_RK, G_RV, G_RG, G_CQ, G_CF, G_CI, G_CG, G_MA, G_MB, G_MC = range(13)
GROUPS_A = (G_AX, G_AY, G_MA)
GROUPS_B = (G_RQ, G_RK, G_RV, G_RG, G_MB)
GROUPS_C = (G_CQ, G_CF, G_CI, G_CG, G_MC)

VMEM_LIMIT = 56 * 1024 * 1024
SUBLANES = 8
LANES = 128


def _cparams(n_axes):
    return pltpu.CompilerParams(dimension_semantics=("arbitrary",) * n_axes,
                                vmem_limit_bytes=VMEM_LIMIT)


def _rms(x, w):
    return x * lax.rsqrt(jnp.mean(x * x, axis=-1, keepdims=True) + EPS) * w


def _dot(a, b):
    return jnp.dot(a, b, preferred_element_type=F32)


def _dot_nt(a, b):
    return lax.dot_general(a, b, (((1,), (1,)), ((), ())), preferred_element_type=F32)


def _dot_tn(a, b):
    return lax.dot_general(a, b, (((0,), (0,)), ((), ())), preferred_element_type=F32)


def _stacked_state(prev, shape_tail, nb, layer, batch, bidx):
    nd = len(shape_tail)
    out_spec = pl.BlockSpec((None, nb) + shape_tail, lambda *g: (layer, bidx(*g)) + (0,) * nd)
    out_shape = jax.ShapeDtypeStruct((DEPTH, batch) + shape_tail, F32)
    if prev is None:
        return [], [], out_spec, out_shape
    return [pl.BlockSpec(memory_space=pl.ANY)], [prev], out_spec, out_shape


def _grid_maps(fused, n_batch_blocks, nt):
    if not fused:
        return (n_batch_blocks, nt), (lambda b, t: b), (lambda b, t: t), None, None
    n_steps = n_batch_blocks * nt
    nxt = lambda s: jnp.minimum(s + 1, n_steps - 1)
    return ((n_steps,), (lambda s: lax.div(s, nt)), (lambda s: lax.rem(s, nt)),
            (lambda s: lax.div(nxt(s), nt)), (lambda s: lax.rem(nxt(s), nt)))


def _wspecs(layer, groups):
    return [pl.BlockSpec((None, D_MODEL, D_MODEL), lambda *_, g=g: (layer, 0, g), pipeline_mode=pl.Buffered(1))
            for g in groups]


def _project(h, nw_ref, w_refs, dst):
    u = _rms(h, nw_ref[...]).astype(BF16)
    for j, w_ref in enumerate(w_refs):
        dst[:, j * D_MODEL:(j + 1) * D_MODEL] = _dot(u, w_ref[...])


def _project_pieces(h, nw_ref, w_refs, dst, width=256):
    u = _rms(h, nw_ref[...]).astype(BF16)

    def piece(j, c):
        def run():
            dst[:, j * D_MODEL + c:j * D_MODEL + c + width] = _dot(u, w_refs[j][:, c:c + width])
        return run

    return [piece(j, c) for j in range(len(w_refs)) for c in range(0, D_MODEL, width)]


def _fused_run(h_ref, hn_ref, nw_ref, w_refs, z_bufs, squeeze, main):
    s = pl.program_id(0)
    rd = (lambda r: r[0]) if squeeze else (lambda r: r[...])

    @pl.when(s == 0)
    def _():
        _project(rd(h_ref), nw_ref, w_refs, z_bufs[0])

    for parity in range(2):
        @pl.when(lax.rem(s, 2) == parity)
        def _():
            pieces = _project_pieces(rd(hn_ref), nw_ref, w_refs, z_bufs[1 - parity])

            def emit(k=None):
                for _ in range(len(pieces) if k is None else min(k, len(pieces))):
                    pieces.pop(0)()

            main(z_bufs[parity], emit)
            emit()


def _proj_body(h_ref, nw_ref, w_ref, z_ref, u_scr):
    @pl.when(pl.program_id(1) == 0)
    def _():
        u_scr[...] = _rms(h_ref[...], nw_ref[...]).astype(BF16)

    z_ref[...] = _dot(u_scr[...], w_ref[...])


def _proj(h, nw_all, w_all, layer, tm, tn):
    n = h.shape[0]
    ncols = w_all.shape[2]
    return pl.pallas_call(
        _proj_body,
        grid=(n // tm, ncols // tn),
        in_specs=[
            pl.BlockSpec((tm, D_MODEL), lambda i, j: (i, 0)),
            pl.BlockSpec((None, 1, D_MODEL), lambda i, j: (layer, 0, 0)),
            pl.BlockSpec((None, D_MODEL, tn), lambda i, j: (layer, 0, j)),
        ],
        out_specs=pl.BlockSpec((tm, tn), lambda i, j: (i, j)),
        out_shape=jax.ShapeDtypeStruct((n, ncols), F32),
        scratch_shapes=[pltpu.VMEM((tm, D_MODEL), BF16)],
        compiler_params=_cparams(2),
        name="proj",
    )(h, nw_all, w_all)


def _mixa_body(*refs, T, NB, nt, pos0, has_init, n_prev, fused):
    n_in = 6 if fused else 3
    (cw_ref, cb_ref, wa_ref, ba_ref, wx_ref, bx_ref, lam_ref) = refs[n_in:n_in + 7]
    k = n_in + 7
    if has_init:
        conv0_ref, lru0_ref = refs[k:k + 2]
        k += 2
    k += n_prev
    out_ref, conv_out_ref, lru_out_ref, xpad, hcar, a_scr, u_scr, h_scr, g_scr = refs[k:k + 9]
    t = lax.rem(pl.program_id(0), nt) if fused else pl.program_id(1)

    @pl.when(t == 0)
    def _():
        if has_init:
            xpad[:, 5:8, :] = conv0_ref[...]
            hcar[...] = lru0_ref[...]
        else:
            xpad[:, 5:8, :] = jnp.zeros((NB, 3, D_MODEL), F32)
            hcar[...] = jnp.zeros((NB, 1, D_MODEL), F32)

    sp = jax.nn.softplus(-lam_ref[...])
    cw = cw_ref[...]
    cb = cb_ref[...]
    rowpos = pos0 + t * T + lax.broadcasted_iota(jnp.int32, (T, 1), 0)

    RB = min(T, 64)

    def gates_and_inputs(xc, rows, zy, zm):
        xcb = xc.astype(BF16)
        ra = jnp.concatenate(
            [_dot(xcb[:, n * LRU_BW:(n + 1) * LRU_BW], wa_ref[n]) for n in range(LRU_BLOCKS)], axis=1)
        rx = jnp.concatenate(
            [_dot(xcb[:, n * LRU_BW:(n + 1) * LRU_BW], wx_ref[n]) for n in range(LRU_BLOCKS)], axis=1)
        r = jax.nn.sigmoid(ra + ba_ref[...])
        i = jax.nn.sigmoid(rx + bx_ref[...])
        log_a = -LRU_C * r * sp
        a = jnp.exp(log_a)
        th = jnp.tanh(log_a)
        mult = jnp.sqrt(-2.0 * th / (1.0 - th))
        mult = jnp.where(rowpos[rows] == 0, 1.0, mult)
        a_scr[rows, :] = a
        u_scr[rows, :] = xc * i * mult
        g_scr[rows, :] = jax.nn.sigmoid(zm) * jax.nn.gelu(zy)

    def one(ib, col, emit):
        xp = xpad.at[ib]
        if T >= SUBLANES:
            prev = xp[0:SUBLANES, :]
            for rb in range(T // RB):
                rows = slice(rb * RB, (rb + 1) * RB)
                x = col(ib, 0, rows)
                xcat = jnp.concatenate([prev, x], axis=0)
                xc = cb + cw[0:1] * pltpu.roll(xcat, 3, 0)[SUBLANES:]
                xc = xc + cw[1:2] * pltpu.roll(xcat, 2, 0)[SUBLANES:]
                xc = xc + cw[2:3] * pltpu.roll(xcat, 1, 0)[SUBLANES:]
                xc = xc + cw[3:4] * x
                prev = x[RB - SUBLANES:, :]
                gates_and_inputs(xc, rows, col(ib, 1, rows), col(ib, 2, rows))
                nblk, npc = T // RB, 4 * len(GROUPS_A)
                emit(npc * (rb + 1) // nblk - npc * rb // nblk)
            xp[0:SUBLANES, :] = prev
            conv_out_ref[ib] = prev[5:8, :]
        else:
            rows = slice(0, T)
            x = col(ib, 0, rows)
            xp[8:8 + T, :] = x
            xc = cb + cw[0:1] * xp[5:5 + T, :]
            xc = xc + cw[1:2] * xp[6:6 + T, :]
            xc = xc + cw[2:3] * xp[7:7 + T, :]
            xc = xc + cw[3:4] * x
            tail = xp[T + 5:T + 8, :]
            xp[5:8, :] = tail
            conv_out_ref[ib] = tail
            gates_and_inputs(xc, rows, col(ib, 1, rows), col(ib, 2, rows))

        def step(s, h):
            h = a_scr[pl.ds(s, 1), :] * h + u_scr[pl.ds(s, 1), :]
            h_scr[pl.ds(s, 1), :] = h
            return h

        h_last = lax.fori_loop(0, T, step, hcar[ib], unroll=min(T, 8))
        hcar[ib] = h_last
        lru_out_ref[ib] = h_last
        out_ref[ib] = g_scr[...] * h_scr[...]

    def main(zc, emit):
        if fused:
            col = lambda ib, j, rows: zc[rows, j * D_MODEL:(j + 1) * D_MODEL]
        else:
            col = lambda ib, j, rows: refs[j][ib, rows, :]
            emit = lambda k: None
        if NB == 1:
            one(0, col, emit)
        else:
            def body(ib, c):
                one(ib, col, emit)
                return c
            lax.fori_loop(0, NB, body, 0)

    if fused:
        _fused_run(refs[0], refs[1], refs[2], refs[3:6], refs[k + 9:k + 11], True, main)
    else:
        main(None, None)


def _mixa(src, P, layer, init, prev, B, L, T, NB, pos0, fused):
    has_init = init is not None
    nt = L // T
    grid, bidx, tidx, bnxt, tnxt = _grid_maps(fused, B // NB, nt)
    blk = lambda bf, tf, g=0: pl.BlockSpec((NB, T, D_MODEL), lambda *a: (bf(*a), tf(*a), g))
    pspec = lambda shape: pl.BlockSpec((None,) + shape, lambda *a: (layer,) + (0,) * len(shape))

    if fused:
        in_specs = [blk(bidx, tidx), blk(bnxt, tnxt), pspec((1, D_MODEL))] + _wspecs(layer, GROUPS_A)
        args = [src, src, P["norm_mix_w"]] + [P["w_in"]] * 3
    else:
        in_specs = [blk(bidx, tidx, g) for g in GROUPS_A]
        args = [src] * 3
    in_specs += [pspec((CONV_W, D_MODEL)), pspec((1, D_MODEL)),
                 pspec((LRU_BLOCKS, LRU_BW, LRU_BW)), pspec((1, D_MODEL)),
                 pspec((LRU_BLOCKS, LRU_BW, LRU_BW)), pspec((1, D_MODEL)), pspec((1, D_MODEL))]
    args += [P["conv_w"], P["conv_b"], P["lru_wa"], P["lru_ba"], P["lru_wx"], P["lru_bx"], P["lru_lambda"]]
    if has_init:
        conv0, lru0 = init
        in_specs += [pl.BlockSpec((None, NB, CONV_W - 1, D_MODEL), lambda *a: (layer, bidx(*a), 0, 0)),
                     pl.BlockSpec((None, NB, 1, D_MODEL), lambda *a: (layer, bidx(*a), 0, 0))]
        args += [conv0, lru0]
    pc, pl_ = (None, None) if prev is None else prev
    xs1, xa1, conv_spec, conv_shape = _stacked_state(pc, (CONV_W - 1, D_MODEL), NB, layer, B, bidx)
    xs2, xa2, lru_spec, lru_shape = _stacked_state(pl_, (1, D_MODEL), NB, layer, B, bidx)
    aliases = {} if prev is None else {len(args): 1, len(args) + 1: 2}
    in_specs += xs1 + xs2
    args += xa1 + xa2
    out_specs = [blk(bidx, tidx), conv_spec, lru_spec]
    out_shape = [jax.ShapeDtypeStruct((B, L, D_MODEL), F32), conv_shape, lru_shape]
    scratch = [pltpu.VMEM((NB, T + 8, D_MODEL), F32), pltpu.VMEM((NB, 1, D_MODEL), F32),
               pltpu.VMEM((T, D_MODEL), F32), pltpu.VMEM((T, D_MODEL), F32), pltpu.VMEM((T, D_MODEL), F32),
               pltpu.VMEM((T, D_MODEL), F32)]
    if fused:
        scratch += [pltpu.VMEM((T, len(GROUPS_A) * D_MODEL), F32)] * 2
    return pl.pallas_call(
        functools.partial(_mixa_body, T=T, NB=NB, nt=nt, pos0=pos0, has_init=has_init,
                          n_prev=len(xa1 + xa2), fused=fused),
        grid=grid, in_specs=in_specs, out_specs=out_specs, out_shape=out_shape,
        input_output_aliases=aliases, scratch_shapes=scratch,
        compiler_params=_cparams(len(grid)),
        name="mix_lru",
    )(*args)


def _mixb_body(*refs, NB, nt, has_init, n_prev, fused):
    n_in = 8 if fused else 5
    (cos_ref, sin_ref, d_ref, xi_ref, zeta_ref, gc_ref) = refs[n_in:n_in + 6]
    k = n_in + 6
    if has_init:
        r0_ref = refs[k]
        k += 1
    k += n_prev
    out_ref, r_out_ref, r_scr = refs[k:k + 3]
    t = lax.rem(pl.program_id(0), nt) if fused else pl.program_id(1)
    direct = has_init and nt == 1

    if not direct:
        @pl.when(t == 0)
        def _():
            if has_init:
                r_scr[...] = r0_ref[...]
            else:
                r_scr[...] = jnp.zeros(r_scr.shape, F32)

    cos = cos_ref[...]
    sin = sin_ref[...]
    half = RET_DK // 2

    def rot(x):
        x1, x2 = x[:, :half], x[:, half:]
        return jnp.concatenate([x1 * cos - x2 * sin, x1 * sin + x2 * cos], axis=1)

    def one(ib, col, emit):
        rr = r0_ref.at[ib] if direct else r_scr.at[ib]
        rw = r_out_ref.at[ib] if direct else rr
        for h in range(RET_HEADS):
            sl = slice(h * RET_DK, (h + 1) * RET_DK)
            qr = rot(col(ib, 0, sl))
            kr = rot(col(ib, 1, sl)) * RET_DK ** -0.5
            qb = qr.astype(BF16)
            vb = col(ib, 2, sl).astype(BF16)
            s = _dot_nt(qb, kr.astype(BF16)) * d_ref[h]
            emit(1)
            rh = rr[h]
            o = _dot(s.astype(BF16), vb) + _dot(qb, rh.astype(BF16)) * xi_ref[:, sl]
            emit(1)
            kz = (kr * zeta_ref[:, sl]).astype(BF16)
            rw[h] = gc_ref[:, sl] * rh + _dot_tn(kz, vb)
            emit(1)
            d = o - jnp.mean(o, axis=-1, keepdims=True)
            gn = d * lax.rsqrt(jnp.mean(d * d, axis=-1, keepdims=True) + GN_EPS)
            emit(1)
            out_ref[ib, :, sl] = jax.nn.sigmoid(col(ib, 4, sl)) * (gn * jax.nn.silu(col(ib, 3, sl)))
            emit(1)
        if not direct:
            r_out_ref[ib] = rr[...]

    def main(zc, emit):
        if fused:
            col = lambda ib, j, sl: zc[:, j * D_MODEL + sl.start:j * D_MODEL + sl.stop]
        else:
            col = lambda ib, j, sl: refs[j][ib, :, sl]
            emit = lambda k: None
        if NB == 1:
            one(0, col, emit)
        else:
            def body(ib, c):
                one(ib, col, emit)
                return c
            lax.fori_loop(0, NB, body, 0)

    if fused:
        _fused_run(refs[0], refs[1], refs[2], refs[3:8], refs[k + 3:k + 5], True, main)
    else:
        main(None, None)


def _ret_tables(C, L, pos0):
    t = jnp.arange(C, dtype=F32)
    log_g = jnp.log1p(-jnp.exp2(-5.0 - jnp.arange(RET_HEADS, dtype=F32)))
    rel = t[:, None] - t[None, :]
    dmat = jnp.exp(jnp.where(rel[None] >= 0, rel[None] * log_g[:, None, None], -jnp.inf))
    xi = jnp.exp((t[None, :] + 1.0) * log_g[:, None]).T
    zeta = jnp.exp((C - 1.0 - t[None, :]) * log_g[:, None]).T
    g_c = jnp.exp(C * log_g)
    rep = lambda a: jnp.repeat(a, RET_DV, axis=-1)
    half = RET_DK // 2
    pos = pos0 + jnp.arange(L, dtype=jnp.int32)
    inv = ROPE_BASE ** (-jnp.arange(half, dtype=F32) / half)
    ang = pos.astype(F32)[:, None] * inv[None, :]
    return dict(d=dmat, xi=rep(xi), zeta=rep(zeta), gc=rep(g_c[None, :]), cos=jnp.cos(ang), sin=jnp.sin(ang))


def _mixb(src, P, tab, layer, init, prev, B, L, T, NB, fused):
    has_init = init is not None
    nt = L // T
    grid, bidx, tidx, bnxt, tnxt = _grid_maps(fused, B // NB, nt)
    blk = lambda bf, tf, g=0: pl.BlockSpec((NB, T, D_MODEL), lambda *a: (bf(*a), tf(*a), g))
    const = lambda shape: pl.BlockSpec(shape, lambda *a: (0,) * len(shape))

    if fused:
        in_specs = ([blk(bidx, tidx), blk(bnxt, tnxt), pl.BlockSpec((None, 1, D_MODEL), lambda *a: (layer, 0, 0))]
                    + _wspecs(layer, GROUPS_B))
        args = [src, src, P["norm_mix_w"]] + [P["w_in"]] * 5
    else:
        in_specs = [blk(bidx, tidx, g) for g in GROUPS_B]
        args = [src] * 5
    in_specs += [pl.BlockSpec((T, RET_DK // 2), lambda *a: (tidx(*a), 0)),
                 pl.BlockSpec((T, RET_DK // 2), lambda *a: (tidx(*a), 0)),
                 const((RET_HEADS, T, T)), const((T, D_MODEL)), const((T, D_MODEL)), const((1, D_MODEL))]
    args += [tab["cos"], tab["sin"], tab["d"], tab["xi"], tab["zeta"], tab["gc"]]
    st_shape = (RET_HEADS, RET_DK, RET_DV)
    if has_init:
        in_specs.append(pl.BlockSpec((None, NB) + st_shape, lambda *a: (layer, bidx(*a), 0, 0, 0)))
        args.append(init)
    xs, xa, st_spec, st_out = _stacked_state(prev, st_shape, NB, layer, B, bidx)
    aliases = {} if prev is None else {len(args): 1}
    in_specs += xs
    args += xa
    out_specs = [blk(bidx, tidx), st_spec]
    out_shape = [jax.ShapeDtypeStruct((B, L, D_MODEL), F32), st_out]
    scratch = [pltpu.VMEM(((1 if has_init and nt == 1 else NB),) + st_shape, F32)]
    if fused:
        scratch += [pltpu.VMEM((T, len(GROUPS_B) * D_MODEL), F32)] * 2
    return pl.pallas_call(
        functools.partial(_mixb_body, NB=NB, nt=nt, has_init=has_init, n_prev=len(xa), fused=fused),
        grid=grid, in_specs=in_specs, out_specs=out_specs, out_shape=out_shape,
        input_output_aliases=aliases, scratch_shapes=scratch,
        compiler_params=_cparams(len(grid)),
        name="mix_ret",
    )(*args)


def _hg_levels(lseg):
    ms = []
    m = lseg // 2
    while m >= 1:
        ms.append(m)
        m //= 2
    return ms


def _hg_tables(C, lseg):
    t = np.arange(C)
    masks, sgn = [], []
    for m in _hg_levels(lseg):
        blk = 2 * m
        same = (t[:, None] // blk) == (t[None, :] // blk)
        masks.append(same & ((t[:, None] % blk) >= m) & ((t[None, :] % blk) < m))
        if m < SUBLANES:
            sgn.append(np.where((t % blk) >= m, 1.0, -1.0)[:, None] * np.ones((1, LANES)))
    masks.append(np.eye(C, dtype=bool))
    masks = np.stack(masks).astype(np.float32)
    masks2 = np.concatenate([masks, masks], axis=2)
    seg = t // lseg
    ltri = ((seg[:, None] == seg[None, :]) & (t[None, :] <= t[:, None])).astype(np.float32)
    sgn = np.stack(sgn).astype(np.float32) if sgn else np.ones((1, C, LANES), np.float32)
    return dict(masks=jnp.asarray(masks2), ltri=jnp.asarray(ltri).astype(BF16), sgn=jnp.asarray(sgn))


def _row_bcast(ref, C, blk, off):
    W = ref.shape[1]
    if blk >= SUBLANES:
        pieces = [jnp.broadcast_to(ref[j * blk + off:j * blk + off + 1, :], (blk, W)) for j in range(C // blk)]
    else:
        rowi = lax.broadcasted_iota(jnp.int32, (SUBLANES, W), 0)
        pieces = []
        for g in range(C // SUBLANES):
            acc = None
            for j in range(SUBLANES // blk):
                r = g * SUBLANES + j * blk + off
                bc = jnp.broadcast_to(ref[r:r + 1, :], (SUBLANES, W))
                acc = bc if acc is None else jnp.where(rowi >= j * blk, bc, acc)
            pieces.append(acc)
    return pieces[0] if len(pieces) == 1 else jnp.concatenate(pieces, axis=0)


def _mixc_body(*refs, RT, C, lseg, nt, layer, has_init, n_prev, fused):
    n_in = 8 if fused else 5
    (lb_ref, nw_ref, ltri_ref, mask_ref, sgn_ref) = refs[n_in:n_in + 5]
    k = n_in + 5
    if has_init:
        s0_ref = refs[k]
        k += 1
    k += n_prev
    out_ref, s_out_ref, st_scr, b_scr, qe_scr, ke_scr, oi_scr, bh_scr = refs[k:k + 8]
    t = lax.rem(pl.program_id(0), nt) if fused else pl.program_id(1)
    NS = C // lseg
    NP = HG_HEADS // 2
    PW = 2 * HG_DK

    @pl.when(t == 0)
    def _():
        if has_init:
            st_scr[...] = s0_ref[...]
        else:
            st_scr[...] = jnp.zeros(st_scr.shape, F32)

    if layer > 0:
        x = lb_ref[...]
        e = jnp.exp(x - jnp.max(x, axis=0, keepdims=True))
        sm = e / jnp.sum(e, axis=0, keepdims=True)
        lb = sm[1:2]
        for j in range(2, layer + 1):
            lb = lb + sm[j:j + 1]
    levels = _hg_levels(lseg)

    def blockdiag_rows(pair):
        z = jnp.zeros((pair.shape[0], HG_DK), pair.dtype)
        return jnp.concatenate([jnp.concatenate([pair[:, :HG_DK], z], axis=1),
                                jnp.concatenate([z, pair[:, HG_DK:]], axis=1)], axis=0)

    def front(r0, col, colh, emit):
        rows = pl.ds(r0, C)
        cf = col(1, rows)
        if layer == 0:
            logf = jax.nn.log_sigmoid(cf)
            kk = jax.nn.sigmoid(-cf)
        else:
            sg = jax.nn.sigmoid(cf)
            logf = jnp.log(lb + (1.0 - lb) * sg)
            kk = (1.0 - lb) * (1.0 - sg)
        qq = jax.nn.silu(col(0, rows))
        vb = col(2, rows).astype(BF16)
        hi = logf.astype(BF16)
        r1 = logf - hi.astype(F32)
        mid = r1.astype(BF16)
        lo = (r1 - mid.astype(F32)).astype(BF16)
        lt = ltri_ref[...]
        b2 = ((_dot(lt, lo) + _dot(lt, mid)) + _dot(lt, hi)) * LOG2E
        b_scr[...] = b2
        bend = _row_bcast(b_scr, C, lseg, lseg - 1)
        qe = qq * jnp.exp2(b2)
        ke = kk * jnp.exp2(bend - b2)
        qb = qq.astype(BF16)
        kb = kk.astype(BF16)
        emit(1)

        def level_w(li, m):
            blk = 2 * m
            if m >= SUBLANES:
                pieces = []
                for j in range(C // blk):
                    lo = slice(j * blk, j * blk + m)
                    up = slice(j * blk + m, (j + 1) * blk)
                    brow = b_scr[j * blk + m - 1:j * blk + m, :]
                    pieces.append(kk[lo] * jnp.exp2(brow - b2[lo]))
                    pieces.append(qq[up] * jnp.exp2(b2[up] - brow))
                w = jnp.concatenate(pieces, axis=0)
            else:
                si = li - (len(levels) - sgn_ref.shape[0])
                sgn = jnp.concatenate([sgn_ref[si]] * (D_MODEL // LANES), axis=1)
                e = jnp.exp2((b2 - _row_bcast(b_scr, C, blk, m - 1)) * sgn)
                w = jnp.where(sgn > 0.0, qq, kk) * e
            return w.astype(BF16)

        amat = [mask_ref[len(levels)] * _dot_nt(qb[:, p * PW:(p + 1) * PW], blockdiag_rows(kb[:, p * PW:(p + 1) * PW]))
                for p in range(NP)]
        for li, m in enumerate(levels):
            w = level_w(li, m)
            for p in range(NP):
                wp = w[:, p * PW:(p + 1) * PW]
                amat[p] = amat[p] + mask_ref[li] * _dot_nt(wp, blockdiag_rows(wp))
            if li in ((0, 2, 4) if RT // C <= 4 else (2,)):
                emit(1)
        bcol = b_scr[C - SUBLANES:C, :] if NS == 1 else None
        return dict(r0=r0, rows=rows, amat=amat, vb=vb, qe=qe, ke=ke, b2=b2, bcol=bcol)

    def back(v, col, colh, emit):
        r0, rows, amat, vb, qe, ke, b2, bcol = (v[k] for k in ("r0", "rows", "amat", "vb", "qe", "ke", "b2", "bcol"))
        o_intra = jnp.concatenate(
            [_dot(amat[p].astype(BF16), blockdiag_rows(vb[:, p * PW:(p + 1) * PW])) for p in range(NP)], axis=1)

        if NS == 1:
            o_parts = []
            for p in range(NP):
                sbd = blockdiag_rows(jnp.concatenate([st_scr[0, 2 * p], st_scr[0, 2 * p + 1]], axis=1))
                o_parts.append(_dot(qe[:, p * PW:(p + 1) * PW].astype(BF16), sbd.astype(BF16)))
            o_inter = jnp.concatenate(o_parts, axis=1)
            emit(1)
            keb = ke.astype(BF16)
            for h in range(HG_HEADS):
                sl = slice(h * HG_DK, (h + 1) * HG_DK)
                dec = jnp.exp2(bcol[:, sl].T[:, SUBLANES - 1:SUBLANES])
                st_scr[0, h] = dec * st_scr[0, h] + _dot_tn(keb[:, sl], vb[:, sl])
        else:
            qe_scr[...] = qe
            ke_scr[...] = ke
            for h in range(HG_HEADS):
                sl = slice(h * HG_DK, (h + 1) * HG_DK)
                bh_scr[h] = b2[:, sl]
                bends = bh_scr[h, pl.ds(lseg - 1, NS, stride=lseg), :]
                dec_all = jnp.exp2(bends.T)
                for s in range(NS):
                    rs = slice(s * lseg, (s + 1) * lseg)
                    st = st_scr[s, h]
                    oi_scr[rs, sl] = _dot(qe_scr[rs, sl].astype(BF16), st.astype(BF16))
                    st_scr[s, h] = dec_all[:, s:s + 1] * st + _dot_tn(
                        ke_scr[rs, sl].astype(BF16), colh(2, pl.ds(r0 + s * lseg, lseg), sl).astype(BF16))
            o_inter = oi_scr[...]

        o = o_inter + o_intra
        gate = jax.nn.sigmoid(col(4, rows))
        og = nw_ref[...] * jax.nn.silu(col(3, rows))
        for h in range(HG_HEADS):
            sl = slice(h * HG_DK, (h + 1) * HG_DK)
            oh = o[:, sl]
            rms = oh * lax.rsqrt(jnp.mean(oh * oh, axis=-1, keepdims=True) + EPS)
            out_ref[rows, sl] = gate[:, sl] * (rms * og[:, sl])

    def chunk(r0, col, colh, emit):
        back(front(r0, col, colh, emit), col, colh, emit)

    def main(zc, emit):
        if fused:
            col = lambda j, rows: zc[rows, j * D_MODEL:(j + 1) * D_MODEL]
            colh = lambda j, rows, sl: zc[rows, j * D_MODEL + sl.start:j * D_MODEL + sl.stop]
            nch = RT // C
            v = front(0, col, colh, emit)
            for c in range(nch):
                vn = front((c + 1) * C, col, colh, emit) if c + 1 < nch else None
                back(v, col, colh, emit)
                v = vn
            return
        col = lambda j, rows: refs[j][rows, :]
        colh = lambda j, rows, sl: refs[j][rows, sl]
        emit = lambda k: None
        if RT == C:
            chunk(0, col, colh, emit)
        else:
            def cbody(c, carry):
                chunk(pl.multiple_of(c * C, C), col, colh, emit)
                return carry
            lax.fori_loop(0, RT // C, cbody, 0)

    if fused:
        _fused_run(refs[0], refs[1], refs[2], refs[3:8], refs[k + 8:k + 10], False, main)
    else:
        main(None, None)

    s_out_ref[...] = st_scr[...]


def _mixc(src, P, tab, layer, init, prev, B, L, RT, C, lseg, fused):
    n = B * L
    has_init = init is not None
    NS = C // lseg
    nt = L // RT if lseg == C else 1
    nlev = len(_hg_levels(lseg)) + 1
    grid, bidx, tidx, bnxt, tnxt = _grid_maps(fused, n // (RT * nt), nt)
    blk = lambda bf, tf, g=0: pl.BlockSpec((RT, D_MODEL), lambda *a: (bf(*a) * nt + tf(*a), g))
    const = lambda shape: pl.BlockSpec(shape, lambda *a: (0,) * len(shape))

    if fused:
        in_specs = ([blk(bidx, tidx), blk(bnxt, tnxt), pl.BlockSpec((None, 1, D_MODEL), lambda *a: (layer, 0, 0))]
                    + _wspecs(layer, GROUPS_C))
        args = [src, src, P["norm_mix_w"]] + [P["w_in"]] * 5
    else:
        in_specs = [blk(bidx, tidx, g) for g in GROUPS_C]
        args = [src] * 5
    in_specs += [const((DEPTH, D_MODEL)), pl.BlockSpec((None, 1, D_MODEL), lambda *a: (layer, 0, 0)),
                 const((C, C)), const((nlev, C, 2 * C)), const(tab["sgn"].shape)]
    args += [P["hg_lb"], P["hg_norm_w"], tab["ltri"], tab["masks"], tab["sgn"]]
    st_shape = (HG_HEADS, HG_DK, HG_DV)
    if has_init:
        in_specs.append(pl.BlockSpec((None, NS) + st_shape, lambda *a: (layer, bidx(*a), 0, 0, 0)))
        args.append(init)
    xs, xa, st_spec, st_out = _stacked_state(prev, st_shape, NS, layer, B, bidx)
    aliases = {} if prev is None else {len(args): 1}
    in_specs += xs
    args += xa
    out_specs = [blk(bidx, tidx), st_spec]
    out_shape = [jax.ShapeDtypeStruct((n, D_MODEL), F32), st_out]
    scratch = [pltpu.VMEM((NS,) + st_shape, F32), pltpu.VMEM((C, D_MODEL), F32),
               pltpu.VMEM((C, D_MODEL), F32), pltpu.VMEM((C, D_MODEL), F32),
               pltpu.VMEM((C, D_MODEL), F32), pltpu.VMEM((HG_HEADS, C, HG_DK), F32)]
    if fused:
        scratch += [pltpu.VMEM((RT, len(GROUPS_C) * D_MODEL), F32)] * 2
    return pl.pallas_call(
        functools.partial(_mixc_body, RT=RT, C=C, lseg=lseg, nt=nt, layer=layer, has_init=has_init,
                          n_prev=len(xa), fused=fused),
        grid=grid, in_specs=in_specs, out_specs=out_specs, out_shape=out_shape,
        input_output_aliases=aliases, scratch_shapes=scratch,
        compiler_params=_cparams(len(grid)),
        name="mix_hgrn",
    )(*args)


def _post_body(*refs, final):
    if final:
        (h_ref, a_ref, b_ref, c_ref, wo_ref, nw_ref, wg_ref, wu_ref, wd_ref, nf_ref, hout_ref, y_ref) = refs
    else:
        (h_ref, a_ref, b_ref, c_ref, wo_ref, nw_ref, wg_ref, wu_ref, wd_ref, hout_ref) = refs
    mix = (a_ref[...] + b_ref[...]) + c_ref[...]
    h1 = h_ref[...] + _dot(mix.astype(BF16), wo_ref[...])
    u = _rms(h1, nw_ref[...]).astype(BF16)
    act = (jax.nn.silu(_dot(u, wg_ref[...])) * _dot(u, wu_ref[...])).astype(BF16)
    h2 = h1 + _dot(act, wd_ref[...])
    hout_ref[...] = h2
    if final:
        y_ref[...] = _rms(h2, nf_ref[...])


def _post(h, oa, ob, oc, P, layer, tm, final):
    n = h.shape[0]
    row = pl.BlockSpec((tm, D_MODEL), lambda i: (i, 0))

    def wspec(shape):
        return pl.BlockSpec((None,) + shape, lambda i: (layer, 0, 0), pipeline_mode=pl.Buffered(1))

    in_specs = [row, row, row, row,
                wspec((D_MODEL, D_MODEL)), wspec((1, D_MODEL)),
                wspec((D_MODEL, D_FF)), wspec((D_MODEL, D_FF)), wspec((D_FF, D_MODEL))]
    args = [h, oa, ob, oc, P["w_out"], P["norm_ffn_w"], P["w_gate"], P["w_up"], P["w_down"]]
    out_specs = [row]
    out_shape = [jax.ShapeDtypeStruct((n, D_MODEL), F32)]
    if final:
        in_specs.append(pl.BlockSpec((1, D_MODEL), lambda i: (0, 0)))
        args.append(P["norm_final_w"])
        out_specs.append(row)
        out_shape.append(jax.ShapeDtypeStruct((n, D_MODEL), F32))
    return pl.pallas_call(
        functools.partial(_post_body, final=final),
        grid=(n // tm,),
        in_specs=in_specs, out_specs=out_specs, out_shape=out_shape,
        compiler_params=_cparams(1),
        name="post",
    )(*args)


def _trunk_cfg(B, L):
    n = B * L
    if L >= 512:
        return dict(fused=True, tm=min(n, 1024), NB=1, TA=512, TB=256, RC=512, CC=64, LS=64, tp=min(n, 512))
    nbc = min(B, 64 // L)
    return dict(fused=False, tm=min(n, 512), NB=min(B, 8), TA=L, TB=L, RC=nbc * L, CC=nbc * L, LS=L,
                tp=min(n, 512))


def _run_trunk(x, pos0, init, P):
    B, L, _ = x.shape
    cfg = _trunk_cfg(B, L)
    fused = cfg["fused"]
    n = B * L
    h = x.reshape(n, D_MODEL).astype(F32)
    rtab = _ret_tables(cfg["TB"], L, pos0)
    htab = _hg_tables(cfg["CC"], cfg["LS"])
    conv = lru = ret = hg = None
    y = None
    for l in range(DEPTH):
        if fused:
            src2, src3 = h, h.reshape(B, L, D_MODEL)
        else:
            src2 = _proj(h, P["norm_mix_w"], P["w_in"], l, cfg["tm"], IN_COLS // 4)
            src3 = src2.reshape(B, L, IN_COLS)
        ia = None if init is None else (init[0], init[1])
        pa = None if l == 0 else (conv, lru)
        oa, conv, lru = _mixa(src3, P, l, ia, pa, B, L, cfg["TA"], cfg["NB"], pos0, fused)
        ob, ret = _mixb(src3, P, rtab, l, None if init is None else init[2], ret, B, L, cfg["TB"], cfg["NB"], fused)
        oc, hg = _mixc(src2, P, htab, l, None if init is None else init[3], hg, B, L, cfg["RC"], cfg["CC"],
                       cfg["LS"], fused)
        res = _post(h, oa.reshape(n, D_MODEL), ob.reshape(n, D_MODEL), oc, P, l, cfg["tp"], l == DEPTH - 1)
        h = res[0]
        if l == DEPTH - 1:
            y = res[1]
    return y.reshape(B, L, D_MODEL), conv, lru.reshape(DEPTH, B, D_MODEL), ret, hg


def kernel(x_prompt, x_sample, state_conv, state_lru, state_ret, state_hgrn, norm_mix_w, w_in, conv_w, conv_b, lru_wa, lru_ba, lru_wx, lru_bx, lru_lambda, hg_lb, hg_norm_w, w_out, norm_ffn_w, w_gate, w_up, w_down, norm_final_w):
    row = lambda a: a.astype(F32).reshape(DEPTH, 1, D_MODEL)
    P = dict(
        norm_mix_w=row(norm_mix_w), w_in=w_in.astype(BF16),
        conv_w=conv_w.astype(F32), conv_b=row(conv_b),
        lru_wa=lru_wa.astype(BF16), lru_ba=row(lru_ba), lru_wx=lru_wx.astype(BF16), lru_bx=row(lru_bx),
        lru_lambda=row(lru_lambda), hg_lb=hg_lb.astype(F32), hg_norm_w=row(hg_norm_w),
        w_out=w_out.astype(BF16), norm_ffn_w=row(norm_ffn_w),
        w_gate=w_gate.astype(BF16), w_up=w_up.astype(BF16), w_down=w_down.astype(BF16),
        norm_final_w=norm_final_w.astype(F32).reshape(1, D_MODEL))
    init_s = (state_conv.astype(F32), state_lru.astype(F32).reshape(DEPTH, -1, 1, D_MODEL),
              state_ret.astype(F32), state_hgrn.astype(F32))
    y_p, conv_p, lru_p, ret_p, hg_p = _run_trunk(x_prompt, 0, None, P)
    y_s, conv_s, lru_s, ret_s, hg_s = _run_trunk(x_sample, PAST_LEN, init_s, P)
    return (y_p.astype(x_prompt.dtype), y_s.astype(x_sample.dtype),
            conv_p.astype(state_conv.dtype), lru_p.astype(state_lru.dtype),
            ret_p.astype(state_ret.dtype), hg_p.astype(state_hgrn.dtype),
            conv_s.astype(state_conv.dtype), lru_s.astype(state_lru.dtype),
            ret_s.astype(state_ret.dtype), hg_s.astype(state_hgrn.dtype))
```

```python
import functools

import numpy as np
import jax
import jax.numpy as jnp
from jax import lax
from jax.experimental import pallas as pl
from jax.experimental.pallas import tpu as pltpu

F32 = jnp.float32
BF16 = jnp.bfloat16

D_MODEL = 1024
DEPTH = 4
PAST_LEN = 16384
LRU_BLOCKS = 4
LRU_BW = D_MODEL // LRU_BLOCKS
CONV_W = 4
LRU_C = 8.0
RET_HEADS = 4
RET_DK = 256
RET_DV = 256
ROPE_BASE = 10000.0
HG_DK = 128
HG_HEADS = 8
HG_DV = 128
D_FF = 2816
EPS = 1e-6
GN_EPS = 1e-5
LOG2E = 1.4426950408889634
N_GROUPS = 13
IN_COLS = N_GROUPS * D_MODEL
G_AX, G_AY, G_RQ, G_RK, G_RV, G_RG, G_CQ, G_CF, G_CI, G_CG, G_MA, G_MB, G_MC = range(13)
GROUPS_A = (G_AX, G_AY, G_MA)
GROUPS_B = (G_RQ, G_RK, G_RV, G_RG, G_MB)
GROUPS_C = (G_CQ, G_CF, G_CI, G_CG, G_MC)

VMEM_LIMIT = 56 * 1024 * 1024
SUBLANES = 8
LANES = 128


def _cparams(n_axes):
    return pltpu.CompilerParams(dimension_semantics=("arbitrary",) * n_axes,
                                vmem_limit_bytes=VMEM_LIMIT)


def _rms(x, w):
    return x * lax.rsqrt(jnp.mean(x * x, axis=-1, keepdims=True) + EPS) * w


def _dot(a, b):
    return jnp.dot(a, b, preferred_element_type=F32)


def _dot_nt(a, b):
    return lax.dot_general(a, b, (((1,), (1,)), ((), ())), preferred_element_type=F32)


def _dot_tn(a, b):
    return lax.dot_general(a, b, (((0,), (0,)), ((), ())), preferred_element_type=F32)


def _stacked_state(prev, shape_tail, nb, layer, batch, bidx):
    nd = len(shape_tail)
    out_spec = pl.BlockSpec((None, nb) + shape_tail, lambda *g: (layer, bidx(*g)) + (0,) * nd)
    out_shape = jax.ShapeDtypeStruct((DEPTH, batch) + shape_tail, F32)
    if prev is None:
        return [], [], out_spec, out_shape
    return [pl.BlockSpec(memory_space=pl.ANY)], [prev], out_spec, out_shape


def _grid_maps(fused, n_batch_blocks, nt):
    if not fused:
        return (n_batch_blocks, nt), (lambda b, t: b), (lambda b, t: t), None, None
    n_steps = n_batch_blocks * nt
    nxt = lambda s: jnp.minimum(s + 1, n_steps - 1)
    return ((n_steps,), (lambda s: lax.div(s, nt)), (lambda s: lax.rem(s, nt)),
            (lambda s: lax.div(nxt(s), nt)), (lambda s: lax.rem(nxt(s), nt)))


def _wspecs(layer, groups):
    return [pl.BlockSpec((None, D_MODEL, D_MODEL), lambda *_, g=g: (layer, 0, g), pipeline_mode=pl.Buffered(1))
            for g in groups]


def _project(h, nw_ref, w_refs, dst):
    u = _rms(h, nw_ref[...]).astype(BF16)
    for j, w_ref in enumerate(w_refs):
        dst[:, j * D_MODEL:(j + 1) * D_MODEL] = _dot(u, w_ref[...])


def _project_pieces(h, nw_ref, w_refs, dst, width=256):
    u = _rms(h, nw_ref[...]).astype(BF16)

    def piece(j, c):
        def run():
            dst[:, j * D_MODEL + c:j * D_MODEL + c + width] = _dot(u, w_refs[j][:, c:c + width])
        return run

    return [piece(j, c) for j in range(len(w_refs)) for c in range(0, D_MODEL, width)]


def _fused_run(h_ref, hn_ref, nw_ref, w_refs, z_bufs, squeeze, main):
    s = pl.program_id(0)
    rd = (lambda r: r[0]) if squeeze else (lambda r: r[...])

    @pl.when(s == 0)
    def _():
        _project(rd(h_ref), nw_ref, w_refs, z_bufs[0])

    for parity in range(2):
        @pl.when(lax.rem(s, 2) == parity)
        def _():
            pieces = _project_pieces(rd(hn_ref), nw_ref, w_refs, z_bufs[1 - parity])

            def emit(k=None):
                for _ in range(len(pieces) if k is None else min(k, len(pieces))):
                    pieces.pop(0)()

            main(z_bufs[parity], emit)
            emit()


def _proj_body(h_ref, nw_ref, w_ref, z_ref, u_scr):
    @pl.when(pl.program_id(1) == 0)
    def _():
        u_scr[...] = _rms(h_ref[...], nw_ref[...]).astype(BF16)

    z_ref[...] = _dot(u_scr[...], w_ref[...])


def _proj(h, nw_all, w_all, layer, tm, tn):
    n = h.shape[0]
    ncols = w_all.shape[2]
    return pl.pallas_call(
        _proj_body,
        grid=(n // tm, ncols // tn),
        in_specs=[
            pl.BlockSpec((tm, D_MODEL), lambda i, j: (i, 0)),
            pl.BlockSpec((None, 1, D_MODEL), lambda i, j: (layer, 0, 0)),
            pl.BlockSpec((None, D_MODEL, tn), lambda i, j: (layer, 0, j)),
        ],
        out_specs=pl.BlockSpec((tm, tn), lambda i, j: (i, j)),
        out_shape=jax.ShapeDtypeStruct((n, ncols), F32),
        scratch_shapes=[pltpu.VMEM((tm, D_MODEL), BF16)],
        compiler_params=_cparams(2),
        name="proj",
    )(h, nw_all, w_all)


def _mixa_body(*refs, T, NB, nt, pos0, has_init, n_prev, fused):
    n_in = 6 if fused else 3
    (cw_ref, cb_ref, wa_ref, ba_ref, wx_ref, bx_ref, lam_ref) = refs[n_in:n_in + 7]
    k = n_in + 7
    if has_init:
        conv0_ref, lru0_ref = refs[k:k + 2]
        k += 2
    k += n_prev
    out_ref, conv_out_ref, lru_out_ref, xpad, hcar, a_scr, u_scr, h_scr, g_scr = refs[k:k + 9]
    t = lax.rem(pl.program_id(0), nt) if fused else pl.program_id(1)

    @pl.when(t == 0)
    def _():
        if has_init:
            xpad[:, 5:8, :] = conv0_ref[...]
            hcar[...] = lru0_ref[...]
        else:
            xpad[:, 5:8, :] = jnp.zeros((NB, 3, D_MODEL), F32)
            hcar[...] = jnp.zeros((NB, 1, D_MODEL), F32)

    sp = jax.nn.softplus(-lam_ref[...])
    cw = cw_ref[...]
    cb = cb_ref[...]
    rowpos = pos0 + t * T + lax.broadcasted_iota(jnp.int32, (T, 1), 0)

    RB = min(T, 64)

    def gates_and_inputs(xc, rows, zy, zm):
        xcb = xc.astype(BF16)
        ra = jnp.concatenate(
            [_dot(xcb[:, n * LRU_BW:(n + 1) * LRU_BW], wa_ref[n]) for n in range(LRU_BLOCKS)], axis=1)
        rx = jnp.concatenate(
            [_dot(xcb[:, n * LRU_BW:(n + 1) * LRU_BW], wx_ref[n]) for n in range(LRU_BLOCKS)], axis=1)
        r = jax.nn.sigmoid(ra + ba_ref[...])
        i = jax.nn.sigmoid(rx + bx_ref[...])
        log_a = -LRU_C * r * sp
        a = jnp.exp(log_a)
        th = jnp.tanh(log_a)
        mult = jnp.sqrt(-2.0 * th / (1.0 - th))
        mult = jnp.where(rowpos[rows] == 0, 1.0, mult)
        a_scr[rows, :] = a
        u_scr[rows, :] = xc * i * mult
        g_scr[rows, :] = jax.nn.sigmoid(zm) * jax.nn.gelu(zy)

    def one(ib, col, emit):
        xp = xpad.at[ib]
        if T >= SUBLANES:
            prev = xp[0:SUBLANES, :]
            for rb in range(T // RB):
                rows = slice(rb * RB, (rb + 1) * RB)
                x = col(ib, 0, rows)
                xcat = jnp.concatenate([prev, x], axis=0)
                xc = cb + cw[0:1] * pltpu.roll(xcat, 3, 0)[SUBLANES:]
                xc = xc + cw[1:2] * pltpu.roll(xcat, 2, 0)[SUBLANES:]
                xc = xc + cw[2:3] * pltpu.roll(xcat, 1, 0)[SUBLANES:]
                xc = xc + cw[3:4] * x
                prev = x[RB - SUBLANES:, :]
                gates_and_inputs(xc, rows, col(ib, 1, rows), col(ib, 2, rows))
                nblk, npc = T // RB, 4 * len(GROUPS_A)
                emit(npc * (rb + 1) // nblk - npc * rb // nblk)
            xp[0:SUBLANES, :] = prev
            conv_out_ref[ib] = prev[5:8, :]
        else:
            rows = slice(0, T)
            x = col(ib, 0, rows)
            xp[8:8 + T, :] = x
            xc = cb + cw[0:1] * xp[5:5 + T, :]
            xc = xc + cw[1:2] * xp[6:6 + T, :]
            xc = xc + cw[2:3] * xp[7:7 + T, :]
            xc = xc + cw[3:4] * x
            tail = xp[T + 5:T + 8, :]
            xp[5:8, :] = tail
            conv_out_ref[ib] = tail
            gates_and_inputs(xc, rows, col(ib, 1, rows), col(ib, 2, rows))

        def step(s, h):
            h = a_scr[pl.ds(s, 1), :] * h + u_scr[pl.ds(s, 1), :]
            h_scr[pl.ds(s, 1), :] = h
            return h

        h_last = lax.fori_loop(0, T, step, hcar[ib], unroll=min(T, 8))
        hcar[ib] = h_last
        lru_out_ref[ib] = h_last
        out_ref[ib] = g_scr[...] * h_scr[...]

    def main(zc, emit):
        if fused:
            col = lambda ib, j, rows: zc[rows, j * D_MODEL:(j + 1) * D_MODEL]
        else:
            col = lambda ib, j, rows: refs[j][ib, rows, :]
            emit = lambda k: None
        if NB == 1:
            one(0, col, emit)
        else:
            def body(ib, c):
                one(ib, col, emit)
                return c
            lax.fori_loop(0, NB, body, 0)

    if fused:
        _fused_run(refs[0], refs[1], refs[2], refs[3:6], refs[k + 9:k + 11], True, main)
    else:
        main(None, None)


def _mixa(src, P, layer, init, prev, B, L, T, NB, pos0, fused):
    has_init = init is not None
    nt = L // T
    grid, bidx, tidx, bnxt, tnxt = _grid_maps(fused, B // NB, nt)
    blk = lambda bf, tf, g=0: pl.BlockSpec((NB, T, D_MODEL), lambda *a: (bf(*a), tf(*a), g))
    pspec = lambda shape: pl.BlockSpec((None,) + shape, lambda *a: (layer,) + (0,) * len(shape))

    if fused:
        first = pl.BlockSpec((NB, T, D_MODEL), lambda *a: (0, 0, 0), pipeline_mode=pl.Buffered(1))
        in_specs = [first, blk(bnxt, tnxt), pspec((1, D_MODEL))] + _wspecs(layer, GROUPS_A)
        args = [src, src, P["norm_mix_w"]] + [P["w_in"]] * 3
    else:
        in_specs = [blk(bidx, tidx, g) for g in GROUPS_A]
        args = [src] * 3
    in_specs += [pspec((CONV_W, D_MODEL)), pspec((1, D_MODEL)),
                 pspec((LRU_BLOCKS, LRU_BW, LRU_BW)), pspec((1, D_MODEL)),
                 pspec((LRU_BLOCKS, LRU_BW, LRU_BW)), pspec((1, D_MODEL)), pspec((1, D_MODEL))]
    args += [P["conv_w"], P["conv_b"], P["lru_wa"], P["lru_ba"], P["lru_wx"], P["lru_bx"], P["lru_lambda"]]
    if has_init:
        conv0, lru0 = init
        in_specs += [pl.BlockSpec((None, NB, CONV_W - 1, D_MODEL), lambda *a: (layer, bidx(*a), 0, 0)),
                     pl.BlockSpec((None, NB, 1, D_MODEL), lambda *a: (layer, bidx(*a), 0, 0))]
        args += [conv0, lru0]
    pc, pl_ = (None, None) if prev is None else prev
    xs1, xa1, conv_spec, conv_shape = _stacked_state(pc, (CONV_W - 1, D_MODEL), NB, layer, B, bidx)
    xs2, xa2, lru_spec, lru_shape = _stacked_state(pl_, (1, D_MODEL), NB, layer, B, bidx)
    aliases = {} if prev is None else {len(args): 1, len(args) + 1: 2}
    in_specs += xs1 + xs2
    args += xa1 + xa2
    out_specs = [blk(bidx, tidx), conv_spec, lru_spec]
    out_shape = [jax.ShapeDtypeStruct((B, L, D_MODEL), F32), conv_shape, lru_shape]
    scratch = [pltpu.VMEM((NB, T + 8, D_MODEL), F32), pltpu.VMEM((NB, 1, D_MODEL), F32),
               pltpu.VMEM((T, D_MODEL), F32), pltpu.VMEM((T, D_MODEL), F32), pltpu.VMEM((T, D_MODEL), F32),
               pltpu.VMEM((T, D_MODEL), F32)]
    if fused:
        scratch += [pltpu.VMEM((T, len(GROUPS_A) * D_MODEL), F32)] * 2
    return pl.pallas_call(
        functools.partial(_mixa_body, T=T, NB=NB, nt=nt, pos0=pos0, has_init=has_init,
                          n_prev=len(xa1 + xa2), fused=fused),
        grid=grid, in_specs=in_specs, out_specs=out_specs, out_shape=out_shape,
        input_output_aliases=aliases, scratch_shapes=scratch,
        compiler_params=_cparams(len(grid)),
        name="mix_lru",
    )(*args)


def _mixb_body(*refs, NB, nt, has_init, n_prev, fused):
    n_in = 8 if fused else 5
    (cos_ref, sin_ref, d_ref, xi_ref, zeta_ref, gc_ref) = refs[n_in:n_in + 6]
    k = n_in + 6
    if has_init:
        r0_ref = refs[k]
        k += 1
    k += n_prev
    out_ref, r_out_ref, r_scr = refs[k:k + 3]
    t = lax.rem(pl.program_id(0), nt) if fused else pl.program_id(1)
    direct = has_init and nt == 1

    if not direct:
        @pl.when(t == 0)
        def _():
            if has_init:
                r_scr[...] = r0_ref[...]
            else:
                r_scr[...] = jnp.zeros(r_scr.shape, F32)

    cos = cos_ref[...]
    sin = sin_ref[...]
    half = RET_DK // 2

    def rot(x):
        x1, x2 = x[:, :half], x[:, half:]
        return jnp.concatenate([x1 * cos - x2 * sin, x1 * sin + x2 * cos], axis=1)

    def one(ib, col, emit):
        rr = r0_ref.at[ib] if direct else r_scr.at[ib]
        rw = r_out_ref.at[ib] if direct else rr
        for h in range(RET_HEADS):
            sl = slice(h * RET_DK, (h + 1) * RET_DK)
            qr = rot(col(ib, 0, sl))
            kr = rot(col(ib, 1, sl)) * RET_DK ** -0.5
            qb = qr.astype(BF16)
            vb = col(ib, 2, sl).astype(BF16)
            s = _dot_nt(qb, kr.astype(BF16)) * d_ref[h]
            emit(1)
            rh = rr[h]
            o = _dot(s.astype(BF16), vb) + _dot(qb, rh.astype(BF16)) * xi_ref[:, sl]
            emit(1)
            kz = (kr * zeta_ref[:, sl]).astype(BF16)
            rw[h] = gc_ref[:, sl] * rh + _dot_tn(kz, vb)
            emit(1)
            d = o - jnp.mean(o, axis=-1, keepdims=True)
            gn = d * lax.rsqrt(jnp.mean(d * d, axis=-1, keepdims=True) + GN_EPS)
            emit(1)
            out_ref[ib, :, sl] = jax.nn.sigmoid(col(ib, 4, sl)) * (gn * jax.nn.silu(col(ib, 3, sl)))
            emit(1)
        if not direct:
            r_out_ref[ib] = rr[...]

    def main(zc, emit):
        if fused:
            col = lambda ib, j, sl: zc[:, j * D_MODEL + sl.start:j * D_MODEL + sl.stop]
        else:
            col = lambda ib, j, sl: refs[j][ib, :, sl]
            emit = lambda k: None
        if NB == 1:
            one(0, col, emit)
        else:
            def body(ib, c):
                one(ib, col, emit)
                return c
            lax.fori_loop(0, NB, body, 0)

    if fused:
        _fused_run(refs[0], refs[1], refs[2], refs[3:8], refs[k + 3:k + 5], True, main)
    else:
        main(None, None)


def _ret_tables(C, L, pos0):
    t = jnp.arange(C, dtype=F32)
    log_g = jnp.log1p(-jnp.exp2(-5.0 - jnp.arange(RET_HEADS, dtype=F32)))
    rel = t[:, None] - t[None, :]
    dmat = jnp.exp(jnp.where(rel[None] >= 0, rel[None] * log_g[:, None, None], -jnp.inf))
    xi = jnp.exp((t[None, :] + 1.0) * log_g[:, None]).T
    zeta = jnp.exp((C - 1.0 - t[None, :]) * log_g[:, None]).T
    g_c = jnp.exp(C * log_g)
    rep = lambda a: jnp.repeat(a, RET_DV, axis=-1)
    half = RET_DK // 2
    pos = pos0 + jnp.arange(L, dtype=jnp.int32)
    inv = ROPE_BASE ** (-jnp.arange(half, dtype=F32) / half)
    ang = pos.astype(F32)[:, None] * inv[None, :]
    return dict(d=dmat, xi=rep(xi), zeta=rep(zeta), gc=rep(g_c[None, :]), cos=jnp.cos(ang), sin=jnp.sin(ang))


def _mixb(src, P, tab, layer, init, prev, B, L, T, NB, fused):
    has_init = init is not None
    nt = L // T
    grid, bidx, tidx, bnxt, tnxt = _grid_maps(fused, B // NB, nt)
    blk = lambda bf, tf, g=0: pl.BlockSpec((NB, T, D_MODEL), lambda *a: (bf(*a), tf(*a), g))
    const = lambda shape: pl.BlockSpec(shape, lambda *a: (0,) * len(shape))

    if fused:
        first = pl.BlockSpec((NB, T, D_MODEL), lambda *a: (0, 0, 0), pipeline_mode=pl.Buffered(1))
        in_specs = ([first, blk(bnxt, tnxt), pl.BlockSpec((None, 1, D_MODEL), lambda *a: (layer, 0, 0))]
                    + _wspecs(layer, GROUPS_B))
        args = [src, src, P["norm_mix_w"]] + [P["w_in"]] * 5
    else:
        in_specs = [blk(bidx, tidx, g) for g in GROUPS_B]
        args = [src] * 5
    in_specs += [pl.BlockSpec((T, RET_DK // 2), lambda *a: (tidx(*a), 0)),
                 pl.BlockSpec((T, RET_DK // 2), lambda *a: (tidx(*a), 0)),
                 const((RET_HEADS, T, T)), const((T, D_MODEL)), const((T, D_MODEL)), const((1, D_MODEL))]
    args += [tab["cos"], tab["sin"], tab["d"], tab["xi"], tab["zeta"], tab["gc"]]
    st_shape = (RET_HEADS, RET_DK, RET_DV)
    if has_init:
        in_specs.append(pl.BlockSpec((None, NB) + st_shape, lambda *a: (layer, bidx(*a), 0, 0, 0)))
        args.append(init)
    xs, xa, st_spec, st_out = _stacked_state(prev, st_shape, NB, layer, B, bidx)
    aliases = {} if prev is None else {len(args): 1}
    in_specs += xs
    args += xa
    out_specs = [blk(bidx, tidx), st_spec]
    out_shape = [jax.ShapeDtypeStruct((B, L, D_MODEL), F32), st_out]
    scratch = [pltpu.VMEM(((1 if has_init and nt == 1 else NB),) + st_shape, F32)]
    if fused:
        scratch += [pltpu.VMEM((T, len(GROUPS_B) * D_MODEL), F32)] * 2
    return pl.pallas_call(
        functools.partial(_mixb_body, NB=NB, nt=nt, has_init=has_init, n_prev=len(xa), fused=fused),
        grid=grid, in_specs=in_specs, out_specs=out_specs, out_shape=out_shape,
        input_output_aliases=aliases, scratch_shapes=scratch,
        compiler_params=_cparams(len(grid)),
        name="mix_ret",
    )(*args)


def _hg_levels(lseg):
    ms = []
    m = lseg // 2
    while m >= 1:
        ms.append(m)
        m //= 2
    return ms


def _hg_tables(C, lseg):
    t = np.arange(C)
    masks, sgn = [], []
    for m in _hg_levels(lseg):
        blk = 2 * m
        same = (t[:, None] // blk) == (t[None, :] // blk)
        masks.append(same & ((t[:, None] % blk) >= m) & ((t[None, :] % blk) < m))
        if m < SUBLANES:
            sgn.append(np.where((t % blk) >= m, 1.0, -1.0)[:, None] * np.ones((1, LANES)))
    masks.append(np.eye(C, dtype=bool))
    masks = np.stack(masks).astype(np.float32)
    masks2 = np.concatenate([masks, masks], axis=2)
    seg = t // lseg
    ltri = ((seg[:, None] == seg[None, :]) & (t[None, :] <= t[:, None])).astype(np.float32)
    sgn = np.stack(sgn).astype(np.float32) if sgn else np.ones((1, C, LANES), np.float32)
    return dict(masks=jnp.asarray(masks2), ltri=jnp.asarray(ltri).astype(BF16), sgn=jnp.asarray(sgn))


def _row_bcast(ref, C, blk, off):
    W = ref.shape[1]
    if blk >= SUBLANES:
        pieces = [jnp.broadcast_to(ref[j * blk + off:j * blk + off + 1, :], (blk, W)) for j in range(C // blk)]
    else:
        rowi = lax.broadcasted_iota(jnp.int32, (SUBLANES, W), 0)
        pieces = []
        for g in range(C // SUBLANES):
            acc = None
            for j in range(SUBLANES // blk):
                r = g * SUBLANES + j * blk + off
                bc = jnp.broadcast_to(ref[r:r + 1, :], (SUBLANES, W))
                acc = bc if acc is None else jnp.where(rowi >= j * blk, bc, acc)
            pieces.append(acc)
    return pieces[0] if len(pieces) == 1 else jnp.concatenate(pieces, axis=0)


def _mixc_body(*refs, RT, C, lseg, nt, layer, has_init, n_prev, fused):
    n_in = 8 if fused else 5
    (lb_ref, nw_ref, ltri_ref, mask_ref, sgn_ref) = refs[n_in:n_in + 5]
    k = n_in + 5
    if has_init:
        s0_ref = refs[k]
        k += 1
    k += n_prev
    out_ref, s_out_ref, st_scr, b_scr, qe_scr, ke_scr, oi_scr, bh_scr = refs[k:k + 8]
    t = lax.rem(pl.program_id(0), nt) if fused else pl.program_id(1)
    NS = C // lseg
    NP = HG_HEADS // 2
    PW = 2 * HG_DK

    @pl.when(t == 0)
    def _():
        if has_init:
            st_scr[...] = s0_ref[...]
        else:
            st_scr[...] = jnp.zeros(st_scr.shape, F32)

    if layer > 0:
        x = lb_ref[...]
        e = jnp.exp(x - jnp.max(x, axis=0, keepdims=True))
        sm = e / jnp.sum(e, axis=0, keepdims=True)
        lb = sm[1:2]
        for j in range(2, layer + 1):
            lb = lb + sm[j:j + 1]
    levels = _hg_levels(lseg)

    def blockdiag_rows(pair):
        z = jnp.zeros((pair.shape[0], HG_DK), pair.dtype)
        return jnp.concatenate([jnp.concatenate([pair[:, :HG_DK], z], axis=1),
                                jnp.concatenate([z, pair[:, HG_DK:]], axis=1)], axis=0)

    def front(r0, col, colh, emit):
        rows = pl.ds(r0, C)
        cf = col(1, rows)
        if layer == 0:
            logf = jax.nn.log_sigmoid(cf)
            kk = jax.nn.sigmoid(-cf)
        else:
            sg = jax.nn.sigmoid(cf)
            logf = jnp.log(lb + (1.0 - lb) * sg)
            kk = (1.0 - lb) * (1.0 - sg)
        qq = jax.nn.silu(col(0, rows))
        vb = col(2, rows).astype(BF16)
        hi = logf.astype(BF16)
        r1 = logf - hi.astype(F32)
        mid = r1.astype(BF16)
        lo = (r1 - mid.astype(F32)).astype(BF16)
        lt = ltri_ref[...]
        b2 = ((_dot(lt, lo) + _dot(lt, mid)) + _dot(lt, hi)) * LOG2E
        b_scr[...] = b2
        bend = _row_bcast(b_scr, C, lseg, lseg - 1)
        qe = qq * jnp.exp2(b2)
        ke = kk * jnp.exp2(bend - b2)
        qb = qq.astype(BF16)
        kb = kk.astype(BF16)
        emit(1)

        def level_w(li, m):
            blk = 2 * m
            if m >= SUBLANES:
                pieces = []
                for j in range(C // blk):
                    lo = slice(j * blk, j * blk + m)
                    up = slice(j * blk + m, (j + 1) * blk)
                    brow = b_scr[j * blk + m - 1:j * blk + m, :]
                    pieces.append(kk[lo] * jnp.exp2(brow - b2[lo]))
                    pieces.append(qq[up] * jnp.exp2(b2[up] - brow))
                w = jnp.concatenate(pieces, axis=0)
            else:
                si = li - (len(levels) - sgn_ref.shape[0])
                sgn = jnp.concatenate([sgn_ref[si]] * (D_MODEL // LANES), axis=1)
                e = jnp.exp2((b2 - _row_bcast(b_scr, C, blk, m - 1)) * sgn)
                w = jnp.where(sgn > 0.0, qq, kk) * e
            return w.astype(BF16)

        amat = [mask_ref[len(levels)] * _dot_nt(qb[:, p * PW:(p + 1) * PW], blockdiag_rows(kb[:, p * PW:(p + 1) * PW]))
                for p in range(NP)]
        for li, m in enumerate(levels):
            w = level_w(li, m)
            for p in range(NP):
                wp = w[:, p * PW:(p + 1) * PW]
                amat[p] = amat[p] + mask_ref[li] * _dot_nt(wp, blockdiag_rows(wp))
            if li % 2 == 0:
                emit(1)
        bcol = b_scr[C - SUBLANES:C, :] if NS == 1 else None
        return dict(r0=r0, rows=rows, amat=amat, vb=vb, qe=qe, ke=ke, b2=b2, bcol=bcol)

    def back(v, col, colh, emit):
        r0, rows, amat, vb, qe, ke, b2, bcol = (v[k] for k in ("r0", "rows", "amat", "vb", "qe", "ke", "b2", "bcol"))
        o_intra = jnp.concatenate(
            [_dot(amat[p].astype(BF16), blockdiag_rows(vb[:, p * PW:(p + 1) * PW])) for p in range(NP)], axis=1)

        if NS == 1:
            o_parts = []
            for p in range(NP):
                sbd = blockdiag_rows(jnp.concatenate([st_scr[0, 2 * p], st_scr[0, 2 * p + 1]], axis=1))
                o_parts.append(_dot(qe[:, p * PW:(p + 1) * PW].astype(BF16), sbd.astype(BF16)))
            o_inter = jnp.concatenate(o_parts, axis=1)
            emit(1)
            keb = ke.astype(BF16)
            for h in range(HG_HEADS):
                sl = slice(h * HG_DK, (h + 1) * HG_DK)
                dec = jnp.exp2(bcol[:, sl].T[:, SUBLANES - 1:SUBLANES])
                st_scr[0, h] = dec * st_scr[0, h] + _dot_tn(keb[:, sl], vb[:, sl])
        else:
            qe_scr[...] = qe
            ke_scr[...] = ke
            for h in range(HG_HEADS):
                sl = slice(h * HG_DK, (h + 1) * HG_DK)
                bh_scr[h] = b2[:, sl]
                bends = bh_scr[h, pl.ds(lseg - 1, NS, stride=lseg), :]
                dec_all = jnp.exp2(bends.T)
                for s in range(NS):
                    rs = slice(s * lseg, (s + 1) * lseg)
                    st = st_scr[s, h]
                    oi_scr[rs, sl] = _dot(qe_scr[rs, sl].astype(BF16), st.astype(BF16))
                    st_scr[s, h] = dec_all[:, s:s + 1] * st + _dot_tn(
                        ke_scr[rs, sl].astype(BF16), colh(2, pl.ds(r0 + s * lseg, lseg), sl).astype(BF16))
            o_inter = oi_scr[...]

        o = o_inter + o_intra
        gate = jax.nn.sigmoid(col(4, rows))
        og = nw_ref[...] * jax.nn.silu(col(3, rows))
        for h in range(HG_HEADS):
            sl = slice(h * HG_DK, (h + 1) * HG_DK)
            oh = o[:, sl]
            rms = oh * lax.rsqrt(jnp.mean(oh * oh, axis=-1, keepdims=True) + EPS)
            out_ref[rows, sl] = gate[:, sl] * (rms * og[:, sl])

    def chunk(r0, col, colh, emit):
        back(front(r0, col, colh, emit), col, colh, emit)

    def main(zc, emit):
        if fused:
            col = lambda j, rows: zc[rows, j * D_MODEL:(j + 1) * D_MODEL]
            colh = lambda j, rows, sl: zc[rows, j * D_MODEL + sl.start:j * D_MODEL + sl.stop]
            nch = RT // C
            v = front(0, col, colh, emit)
            for c in range(nch):
                vn = front((c + 1) * C, col, colh, emit) if c + 1 < nch else None
                back(v, col, colh, emit)
                v = vn
            return
        col = lambda j, rows: refs[j][rows, :]
        colh = lambda j, rows, sl: refs[j][rows, sl]
        emit = lambda k: None
        if RT == C:
            chunk(0, col, colh, emit)
        else:
            def cbody(c, carry):
                chunk(pl.multiple_of(c * C, C), col, colh, emit)
                return carry
            lax.fori_loop(0, RT // C, cbody, 0)

    if fused:
        _fused_run(refs[0], refs[1], refs[2], refs[3:8], refs[k + 8:k + 10], False, main)
    else:
        main(None, None)

    s_out_ref[...] = st_scr[...]


def _mixc(src, P, tab, layer, init, prev, B, L, RT, C, lseg, fused):
    n = B * L
    has_init = init is not None
    NS = C // lseg
    nt = L // RT if lseg == C else 1
    nlev = len(_hg_levels(lseg)) + 1
    grid, bidx, tidx, bnxt, tnxt = _grid_maps(fused, n // (RT * nt), nt)
    blk = lambda bf, tf, g=0: pl.BlockSpec((RT, D_MODEL), lambda *a: (bf(*a) * nt + tf(*a), g))
    const = lambda shape: pl.BlockSpec(shape, lambda *a: (0,) * len(shape))

    if fused:
        first = pl.BlockSpec((RT, D_MODEL), lambda *a: (0, 0), pipeline_mode=pl.Buffered(1))
        in_specs = ([first, blk(bnxt, tnxt), pl.BlockSpec((None, 1, D_MODEL), lambda *a: (layer, 0, 0))]
                    + _wspecs(layer, GROUPS_C))
        args = [src, src, P["norm_mix_w"]] + [P["w_in"]] * 5
    else:
        in_specs = [blk(bidx, tidx, g) for g in GROUPS_C]
        args = [src] * 5
    in_specs += [const((DEPTH, D_MODEL)), pl.BlockSpec((None, 1, D_MODEL), lambda *a: (layer, 0, 0)),
                 const((C, C)), const((nlev, C, 2 * C)), const(tab["sgn"].shape)]
    args += [P["hg_lb"], P["hg_norm_w"], tab["ltri"], tab["masks"], tab["sgn"]]
    st_shape = (HG_HEADS, HG_DK, HG_DV)
    if has_init:
        in_specs.append(pl.BlockSpec((None, NS) + st_shape, lambda *a: (layer, bidx(*a), 0, 0, 0)))
        args.append(init)
    xs, xa, st_spec, st_out = _stacked_state(prev, st_shape, NS, layer, B, bidx)
    aliases = {} if prev is None else {len(args): 1}
    in_specs += xs
    args += xa
    out_specs = [blk(bidx, tidx), st_spec]
    out_shape = [jax.ShapeDtypeStruct((n, D_MODEL), F32), st_out]
    scratch = [pltpu.VMEM((NS,) + st_shape, F32), pltpu.VMEM((C, D_MODEL), F32),
               pltpu.VMEM((C, D_MODEL), F32), pltpu.VMEM((C, D_MODEL), F32),
               pltpu.VMEM((C, D_MODEL), F32), pltpu.VMEM((HG_HEADS, C, HG_DK), F32)]
    if fused:
        scratch += [pltpu.VMEM((RT, len(GROUPS_C) * D_MODEL), F32)] * 2
    return pl.pallas_call(
        functools.partial(_mixc_body, RT=RT, C=C, lseg=lseg, nt=nt, layer=layer, has_init=has_init,
                          n_prev=len(xa), fused=fused),
        grid=grid, in_specs=in_specs, out_specs=out_specs, out_shape=out_shape,
        input_output_aliases=aliases, scratch_shapes=scratch,
        compiler_params=_cparams(len(grid)),
        name="mix_hgrn",
    )(*args)


def _post_body(*refs, final):
    if final:
        (h_ref, a_ref, b_ref, c_ref, wo_ref, nw_ref, wg_ref, wu_ref, wd_ref, nf_ref, hout_ref, y_ref) = refs
    else:
        (h_ref, a_ref, b_ref, c_ref, wo_ref, nw_ref, wg_ref, wu_ref, wd_ref, hout_ref) = refs
    mix = (a_ref[...] + b_ref[...]) + c_ref[...]
    h1 = h_ref[...] + _dot(mix.astype(BF16), wo_ref[...])
    u = _rms(h1, nw_ref[...]).astype(BF16)
    act = (jax.nn.silu(_dot(u, wg_ref[...])) * _dot(u, wu_ref[...])).astype(BF16)
    h2 = h1 + _dot(act, wd_ref[...])
    hout_ref[...] = h2
    if final:
        y_ref[...] = _rms(h2, nf_ref[...])


def _post(h, oa, ob, oc, P, layer, tm, final):
    n = h.shape[0]
    row = pl.BlockSpec((tm, D_MODEL), lambda i: (i, 0))

    def wspec(shape):
        return pl.BlockSpec((None,) + shape, lambda i: (layer, 0, 0), pipeline_mode=pl.Buffered(1))

    in_specs = [row, row, row, row,
                wspec((D_MODEL, D_MODEL)), wspec((1, D_MODEL)),
                wspec((D_MODEL, D_FF)), wspec((D_MODEL, D_FF)), wspec((D_FF, D_MODEL))]
    args = [h, oa, ob, oc, P["w_out"], P["norm_ffn_w"], P["w_gate"], P["w_up"], P["w_down"]]
    out_specs = [row]
    out_shape = [jax.ShapeDtypeStruct((n, D_MODEL), F32)]
    if final:
        in_specs.append(pl.BlockSpec((1, D_MODEL), lambda i: (0, 0)))
        args.append(P["norm_final_w"])
        out_specs.append(row)
        out_shape.append(jax.ShapeDtypeStruct((n, D_MODEL), F32))
    return pl.pallas_call(
        functools.partial(_post_body, final=final),
        grid=(n // tm,),
        in_specs=in_specs, out_specs=out_specs, out_shape=out_shape,
        compiler_params=_cparams(1),
        name="post",
    )(*args)


def _trunk_cfg(B, L):
    n = B * L
    if L >= 512:
        return dict(fused=True, tm=min(n, 1024), NB=1, TA=256, TB=256, RC=256, CC=64, LS=64, tp=min(n, 512))
    nbc = min(B, 64 // L)
    return dict(fused=False, tm=min(n, 512), NB=min(B, 8), TA=L, TB=L, RC=nbc * L, CC=nbc * L, LS=L,
                tp=min(n, 512))


def _run_trunk(x, pos0, init, P):
    B, L, _ = x.shape
    cfg = _trunk_cfg(B, L)
    fused = cfg["fused"]
    n = B * L
    h = x.reshape(n, D_MODEL).astype(F32)
    rtab = _ret_tables(cfg["TB"], L, pos0)
    htab = _hg_tables(cfg["CC"], cfg["LS"])
    conv = lru = ret = hg = None
    y = None
    for l in range(DEPTH):
        if fused:
            src2, src3 = h, h.reshape(B, L, D_MODEL)
        else:
            src2 = _proj(h, P["norm_mix_w"], P["w_in"], l, cfg["tm"], IN_COLS // 4)
            src3 = src2.reshape(B, L, IN_COLS)
        ia = None if init is None else (init[0], init[1])
        pa = None if l == 0 else (conv, lru)
        oa, conv, lru = _mixa(src3, P, l, ia, pa, B, L, cfg["TA"], cfg["NB"], pos0, fused)
        ob, ret = _mixb(src3, P, rtab, l, None if init is None else init[2], ret, B, L, cfg["TB"], cfg["NB"], fused)
        oc, hg = _mixc(src2, P, htab, l, None if init is None else init[3], hg, B, L, cfg["RC"], cfg["CC"],
                       cfg["LS"], fused)
        res = _post(h, oa.reshape(n, D_MODEL), ob.reshape(n, D_MODEL), oc, P, l, cfg["tp"], l == DEPTH - 1)
        h = res[0]
        if l == DEPTH - 1:
            y = res[1]
    return y.reshape(B, L, D_MODEL), conv, lru.reshape(DEPTH, B, D_MODEL), ret, hg


def kernel(x_prompt, x_sample, state_conv, state_lru, state_ret, state_hgrn, norm_mix_w, w_in, conv_w, conv_b, lru_wa, lru_ba, lru_wx, lru_bx, lru_lambda, hg_lb, hg_norm_w, w_out, norm_ffn_w, w_gate, w_up, w_down, norm_final_w):
    row = lambda a: a.astype(F32).reshape(DEPTH, 1, D_MODEL)
    P = dict(
        norm_mix_w=row(norm_mix_w), w_in=w_in.astype(BF16),
        conv_w=conv_w.astype(F32), conv_b=row(conv_b),
        lru_wa=lru_wa.astype(BF16), lru_ba=row(lru_ba), lru_wx=lru_wx.astype(BF16), lru_bx=row(lru_bx),
        lru_lambda=row(lru_lambda), hg_lb=hg_lb.astype(F32), hg_norm_w=row(hg_norm_w),
        w_out=w_out.astype(BF16), norm_ffn_w=row(norm_ffn_w),
        w_gate=w_gate.astype(BF16), w_up=w_up.astype(BF16), w_down=w_down.astype(BF16),
        norm_final_w=norm_final_w.astype(F32).reshape(1, D_MODEL))
    init_s = (state_conv.astype(F32), state_lru.astype(F32).reshape(DEPTH, -1, 1, D_MODEL),
              state_ret.astype(F32), state_hgrn.astype(F32))
    y_p, conv_p, lru_p, ret_p, hg_p = _run_trunk(x_prompt, 0, None, P)
    y_s, conv_s, lru_s, ret_s, hg_s = _run_trunk(x_sample, PAST_LEN, init_s, P)
    return (y_p.astype(x_prompt.dtype), y_s.astype(x_sample.dtype),
            conv_p.astype(state_conv.dtype), lru_p.astype(state_lru.dtype),
            ret_p.astype(state_ret.dtype), hg_p.astype(state_hgrn.dtype),
            conv_s.astype(state_conv.dtype), lru_s.astype(state_lru.dtype),
            ret_s.astype(state_ret.dtype), hg_s.astype(state_hgrn.dtype))
```
